```python
import jax, jax.numpy as jnp
from jax import lax
import numpy as np

D_MODEL = 2048
BATCH = 4
SEQ = 2048
DEPTH = 2

POOL_WINDOWS = (2, 4, 8, 16)
POOL_GROUP = D_MODEL // 8
POOL_WIDTH = POOL_GROUP * len(POOL_WINDOWS)
ATTN_HEADS = 16
ATTN_KV_HEADS = 4
ATTN_HEAD_DIM = 64
ATTN_WINDOW = 128
ATTN_BLOCK = 128
Q_WIDTH = ATTN_HEADS * ATTN_HEAD_DIM
KV_WIDTH = ATTN_KV_HEADS * ATTN_HEAD_DIM
RET_HEADS = 4
RET_KEY_DIM = 256
RET_VALUE_DIM = 512
RET_CHUNK = 128
RQK_WIDTH = RET_HEADS * RET_KEY_DIM
RV_WIDTH = RET_HEADS * RET_VALUE_DIM
FFN_HIDDEN = 5632
RMS_EPS = 1e-6
N_BRANCHES = 3

IN_SPLITS = (POOL_WIDTH, Q_WIDTH, KV_WIDTH, KV_WIDTH, RQK_WIDTH, RQK_WIDTH, RV_WIDTH, RV_WIDTH)
IN_WIDTH = int(sum(IN_SPLITS))
IN_OFFSETS = tuple(int(o) for o in np.cumsum(IN_SPLITS)[:-1])

kernel_name = "hybrid_gated_pool_swa_retention_macaron"


def rms_norm(x, g):
    xf = x.astype(jnp.float32)
    y = xf * lax.rsqrt(jnp.mean(xf * xf, axis=-1, keepdims=True) + RMS_EPS)
    return (y * g.astype(jnp.float32)).astype(x.dtype)


def swiglu(x, w_up, w_down):
    g, u = jnp.split(x @ w_up, 2, axis=-1)
    return (jax.nn.silu(g) * u) @ w_down


def pool_mixer(xp, w_group, scale):
    B, S, _ = xp.shape
    xf = xp.astype(jnp.float32)
    cs = jnp.concatenate([jnp.zeros((B, 1, POOL_WIDTH), jnp.float32), jnp.cumsum(xf, axis=1)], axis=1)
    t = jnp.arange(S)
    groups = []
    for gi, w in enumerate(POOL_WINDOWS):
        sl = slice(gi * POOL_GROUP, (gi + 1) * POOL_GROUP)
        start = jnp.maximum(t + 1 - w, 0)
        window_sum = cs[:, 1:, sl] - cs[:, start, sl]
        count = (t + 1 - start).astype(jnp.float32)[None, :, None]
        groups.append(window_sum / count - xf[:, :, sl])
    pooled = jnp.stack(groups, axis=2).astype(xp.dtype)
    mixed = jnp.einsum('bsgc,gcd->bsgd', pooled, w_group)
    return mixed.reshape(B, S, POOL_WIDTH) * scale


def sliding_window_attention(q, k, v, sinks):
    B, S, _ = q.shape
    NB = S // ATTN_BLOCK
    G = ATTN_HEADS // ATTN_KV_HEADS
    qb = q.reshape(B, NB, ATTN_BLOCK, ATTN_KV_HEADS, G, ATTN_HEAD_DIM)
    kb = k.reshape(B, NB, ATTN_BLOCK, ATTN_KV_HEADS, ATTN_HEAD_DIM)
    vb = v.reshape(B, NB, ATTN_BLOCK, ATTN_KV_HEADS, ATTN_HEAD_DIM)

    def with_prev(xb):
        prev = jnp.concatenate([jnp.zeros_like(xb[:, :1]), xb[:, :-1]], axis=1)
        return jnp.concatenate([prev, xb], axis=2)

    kk, vv = with_prev(kb), with_prev(vb)
    scores = jnp.einsum('bnqhgd,bnkhd->bhgnqk', qb, kk).astype(jnp.float32) * (ATTN_HEAD_DIM ** -0.5)
    qi = jnp.arange(ATTN_BLOCK)[:, None]
    ki = jnp.arange(2 * ATTN_BLOCK)[None, :]
    dist = qi + ATTN_BLOCK - ki
    key_pos = jnp.arange(NB)[:, None, None] * ATTN_BLOCK - ATTN_BLOCK + ki[None]
    valid = (dist >= 0) & (dist < ATTN_WINDOW) & (key_pos >= 0)
    slopes = jnp.exp2(-8.0 * jnp.arange(1, ATTN_HEADS + 1, dtype=jnp.float32) / ATTN_HEADS)
    alibi = -slopes.reshape(ATTN_KV_HEADS, G)[:, :, None, None, None] * dist.astype(jnp.float32)
    scores = jnp.where(valid, scores + alibi, -jnp.inf)
    sink = sinks.astype(jnp.float32).reshape(ATTN_KV_HEADS, G)[:, :, None, None, None]
    m = jnp.maximum(jnp.max(scores, axis=-1, keepdims=True), sink)
    p = jnp.exp(scores - m)
    p = p / (jnp.sum(p, axis=-1, keepdims=True) + jnp.exp(sink - m))
    out = jnp.einsum('bhgnqk,bnkhd->bnqhgd', p.astype(v.dtype), vv)
    return out.reshape(B, S, Q_WIDTH)


def retention(q, k, v):
    B, S = q.shape[0], q.shape[1]
    NC = S // RET_CHUNK
    log_g = jnp.log(1.0 - jnp.exp2(-5.0 - jnp.arange(RET_HEADS, dtype=jnp.float32)))
    k = k * (RET_KEY_DIM ** -0.5)

    def chunks(a):
        return a.reshape(B, NC, RET_CHUNK, RET_HEADS, a.shape[-1]).transpose(1, 0, 3, 2, 4)

    qc, kc, vc = chunks(q), chunks(k), chunks(v)
    pos = jnp.arange(RET_CHUNK, dtype=jnp.float32)
    diff = pos[:, None] - pos[None, :]
    intra = jnp.where(diff >= 0, jnp.exp(log_g[:, None, None] * jnp.maximum(diff, 0.0)), 0.0)
    q_decay = jnp.exp(log_g[:, None] * (pos + 1.0))[..., None]
    k_decay = jnp.exp(log_g[:, None] * (RET_CHUNK - 1.0 - pos))[..., None]
    chunk_decay = jnp.exp(log_g * RET_CHUNK)[:, None, None]

    def step(state, xs):
        qi, ki, vi = xs
        inner = jnp.einsum('bhid,bhjd->bhij', qi, ki) * intra
        o = jnp.einsum('bhij,bhjv->bhiv', inner, vi) + jnp.einsum('bhid,bhdv->bhiv', qi * q_decay, state)
        state = state * chunk_decay + jnp.einsum('bhjd,bhjv->bhdv', ki * k_decay, vi)
        return state, o

    state0 = jnp.zeros((B, RET_HEADS, RET_KEY_DIM, RET_VALUE_DIM), jnp.float32)
    _, o = lax.scan(step, state0, (qc, kc, vc))
    return o.transpose(1, 0, 3, 2, 4).reshape(B, S, RET_HEADS, RET_VALUE_DIM)


def hybrid_mixer(u, w_in, pool_w, pool_scale, attn_sinks, ret_norm,
                 w_pool_out, w_attn_out, w_ret_out, w_gate, w_out):
    B, S, _ = u.shape
    xp, q, k, v, rq, rk, rv, rg = jnp.split(u @ w_in, IN_OFFSETS, axis=-1)
    pool_out = pool_mixer(xp, pool_w, pool_scale) @ w_pool_out
    attn_out = sliding_window_attention(q, k, v, attn_sinks) @ w_attn_out
    f32 = jnp.float32
    o = retention(rq.reshape(B, S, RET_HEADS, RET_KEY_DIM).astype(f32),
                  rk.reshape(B, S, RET_HEADS, RET_KEY_DIM).astype(f32),
                  rv.reshape(B, S, RET_HEADS, RET_VALUE_DIM).astype(f32))
    o = o * lax.rsqrt(jnp.mean(o * o, axis=-1, keepdims=True) + RMS_EPS)
    o = o.reshape(B, S, RV_WIDTH) * ret_norm.astype(f32)
    ret_out = (jax.nn.silu(rg.astype(f32)) * o).astype(u.dtype) @ w_ret_out
    g_pool, g_attn, g_ret = jnp.split(jax.nn.sigmoid((u @ w_gate).astype(f32)).astype(u.dtype), N_BRANCHES, axis=-1)
    merged = g_pool * pool_out + g_attn * attn_out + g_ret * ret_out
    return merged @ w_out


def setup_inputs(seed: int = 0) -> dict:
    key = jax.random.key(seed)
    ks = jax.random.split(key, 32)
    f32 = jnp.float32

    def dense(k, shape, fan_in):
        return jax.random.normal(k, shape, f32) * (fan_in ** -0.5)

    def gain(k, shape):
        return 1.0 + 0.05 * jax.random.normal(k, shape, f32)

    L, D, F = DEPTH, D_MODEL, FFN_HIDDEN
    return {
        "x": jax.random.normal(ks[0], (BATCH, SEQ, D), f32),
        "ffn1_pre": gain(ks[1], (L, D)),
        "ffn1_up": dense(ks[2], (L, D, 2 * F), D),
        "ffn1_down": dense(ks[3], (L, F, D), F),
        "ffn1_post": gain(ks[4], (L, D)),
        "mix_pre": gain(ks[5], (L, D)),
        "w_in": dense(ks[6], (L, D, IN_WIDTH), D),
        "pool_w": dense(ks[7], (L, len(POOL_WINDOWS), POOL_GROUP, POOL_GROUP), POOL_GROUP),
        "pool_scale": gain(ks[8], (L, POOL_WIDTH)),
        "attn_sinks": jax.random.normal(ks[9], (L, ATTN_HEADS), f32),
        "ret_norm": gain(ks[10], (L, RV_WIDTH)),
        "w_pool_out": dense(ks[11], (L, POOL_WIDTH, D), POOL_WIDTH),
        "w_attn_out": dense(ks[12], (L, Q_WIDTH, D), Q_WIDTH),
        "w_ret_out": dense(ks[13], (L, RV_WIDTH, D), RV_WIDTH),
        "w_gate": dense(ks[14], (L, D, N_BRANCHES * D), D),
        "w_out": dense(ks[15], (L, D, D), D),
        "mix_post": gain(ks[16], (L, D)),
        "ffn2_pre": gain(ks[17], (L, D)),
        "ffn2_up": dense(ks[18], (L, D, 2 * F), D),
        "ffn2_down": dense(ks[19], (L, F, D), F),
        "ffn2_post": gain(ks[20], (L, D)),
    }


def reference(x, ffn1_pre, ffn1_up, ffn1_down, ffn1_post, mix_pre, w_in, pool_w, pool_scale,
              attn_sinks, ret_norm, w_pool_out, w_attn_out, w_ret_out, w_gate, w_out, mix_post,
              ffn2_pre, ffn2_up, ffn2_down, ffn2_post):
    h = x
    for l in range(DEPTH):
        h = h + 0.5 * rms_norm(swiglu(rms_norm(h, ffn1_pre[l]), ffn1_up[l], ffn1_down[l]), ffn1_post[l])
        mix = hybrid_mixer(rms_norm(h, mix_pre[l]), w_in[l], pool_w[l], pool_scale[l], attn_sinks[l],
                           ret_norm[l], w_pool_out[l], w_attn_out[l], w_ret_out[l], w_gate[l], w_out[l])
        h = h + rms_norm(mix, mix_post[l])
        h = h + 0.5 * rms_norm(swiglu(rms_norm(h, ffn2_pre[l]), ffn2_up[l], ffn2_down[l]), ffn2_post[l])
    return h
```

```python
import functools

import jax
import jax.numpy as jnp
from jax import lax
from jax.experimental import pallas as pl
from jax.experimental.pallas import tpu as pltpu

F32 = jnp.float32
BF16 = jnp.bfloat16

D_MODEL = 2048
BATCH = 4
SEQ = 2048
DEPTH = 2
TOKENS = BATCH * SEQ

POOL_WINDOWS = (2, 4, 8, 16)
POOL_GROUP = 256
POOL_WIDTH = 1024
POOL_HALO = 16
ATTN_HEADS = 16
ATTN_KV_HEADS = 4
ATTN_GROUP = ATTN_HEADS // ATTN_KV_HEADS
ATTN_HEAD_DIM = 64
ATTN_BLOCK = 128
Q_WIDTH = 1024
KV_WIDTH = 256
RET_HEADS = 4
RET_KEY_DIM = 256
RET_VALUE_DIM = 512
RQK_WIDTH = 1024
RV_WIDTH = 2048
FFN_HIDDEN = 5632
RMS_EPS = 1e-6

OFF_XP = 0
OFF_Q = OFF_XP + POOL_WIDTH
OFF_K = OFF_Q + Q_WIDTH
OFF_V = OFF_K + KV_WIDTH
OFF_RQ = OFF_V + KV_WIDTH
OFF_RK = OFF_RQ + RQK_WIDTH
OFF_RV = OFF_RK + RQK_WIDTH
OFF_RG = OFF_RV + RV_WIDTH
IN_WIDTH = OFF_RG + RV_WIDTH
OFF_GATE = IN_WIDTH
PROJ_WIDTH = IN_WIDTH + 3 * D_MODEL

VMEM_LIMIT_BYTES = 56 * 1024 * 1024

FFN_TM = 512
FFN_TF = 512
PROJ_TM = 1024
PROJ_TN = 512
POOL_TM = 512
RET_CHUNK = 256
MERGE_TM = 512
MERGE_TJ = 256


def _params(*sem):
    return pltpu.CompilerParams(dimension_semantics=sem, vmem_limit_bytes=VMEM_LIMIT_BYTES)


def _rms_scale(x):
    return lax.rsqrt(jnp.mean(x * x, axis=-1, keepdims=True) + RMS_EPS)


def _ffn_kernel(h_ref, pre_ref, wg_ref, wu_ref, wd_ref, post_ref, o_ref, xn_ref, acc_ref):
    j = pl.program_id(1)

    @pl.when(j == 0)
    def _():
        x = h_ref[...]
        xn_ref[...] = (x * _rms_scale(x) * pre_ref[...]).astype(BF16)
        acc_ref[...] = jnp.zeros_like(acc_ref)

    xn = xn_ref[...]
    g = jnp.dot(xn, wg_ref[...], preferred_element_type=F32)
    u = jnp.dot(xn, wu_ref[...], preferred_element_type=F32)
    a = (g * jax.nn.sigmoid(g) * u).astype(BF16)
    acc_ref[...] += jnp.dot(a, wd_ref[...], preferred_element_type=F32)

    @pl.when(j == pl.num_programs(1) - 1)
    def _():
        y = acc_ref[...]
        o_ref[...] = h_ref[...] + 0.5 * (y * _rms_scale(y) * post_ref[...])


def _ffn(h, pre, w_up, w_down, post, layer):
    nf = FFN_HIDDEN // FFN_TF
    return pl.pallas_call(
        _ffn_kernel,
        grid=(TOKENS // FFN_TM, nf),
        in_specs=[
            pl.BlockSpec((FFN_TM, D_MODEL), lambda i, j: (i, 0)),
            pl.BlockSpec((None, 1, D_MODEL), lambda i, j: (layer, 0, 0)),
            pl.BlockSpec((None, D_MODEL, FFN_TF), lambda i, j: (layer, 0, j)),
            pl.BlockSpec((None, D_MODEL, FFN_TF), lambda i, j: (layer, 0, nf + j)),
            pl.BlockSpec((None, FFN_TF, D_MODEL), lambda i, j: (layer, j, 0)),
            pl.BlockSpec((None, 1, D_MODEL), lambda i, j: (layer, 0, 0)),
        ],
        out_specs=pl.BlockSpec((FFN_TM, D_MODEL), lambda i, j: (i, 0)),
        out_shape=jax.ShapeDtypeStruct((TOKENS, D_MODEL), F32),
        scratch_shapes=[pltpu.VMEM((FFN_TM, D_MODEL), BF16), pltpu.VMEM((FFN_TM, D_MODEL), F32)],
        compiler_params=_params("parallel", "arbitrary"),
        name="ffn",
    )(h, pre, w_up, w_up, w_down, post)


def _proj_kernel(h_ref, pre_ref, w_ref, o_ref, xn_ref):
    @pl.when(pl.program_id(1) == 0)
    def _():
        x = h_ref[...]
        xn_ref[...] = (x * _rms_scale(x) * pre_ref[...]).astype(BF16)

    o_ref[...] = jnp.dot(xn_ref[...], w_ref[...], preferred_element_type=F32).astype(o_ref.dtype)


def _proj(h, pre, w_cat, layer):
    return pl.pallas_call(
        _proj_kernel,
        grid=(TOKENS // PROJ_TM, PROJ_WIDTH // PROJ_TN),
        in_specs=[
            pl.BlockSpec((PROJ_TM, D_MODEL), lambda i, j: (i, 0)),
            pl.BlockSpec((None, 1, D_MODEL), lambda i, j: (layer, 0, 0)),
            pl.BlockSpec((None, D_MODEL, PROJ_TN), lambda i, j: (layer, 0, j)),
        ],
        out_specs=pl.BlockSpec((PROJ_TM, PROJ_TN), lambda i, j: (i, j)),
        out_shape=jax.ShapeDtypeStruct((TOKENS, PROJ_WIDTH), BF16),
        scratch_shapes=[pltpu.VMEM((PROJ_TM, D_MODEL), BF16)],
        compiler_params=_params("parallel", "arbitrary"),
        name="proj",
    )(h, pre, w_cat)


def _pool_kernel(xc_ref, xp_ref, pw_ref, scale_ref, o_ref):
    tiles_per_seq = SEQ // POOL_TM
    it = pl.program_id(0) % tiles_per_seq
    cur = xc_ref[...].astype(F32)
    prev = jnp.where(it == 0, 0.0, xp_ref[...].astype(F32))
    ext = jnp.concatenate([prev, cur], axis=0)
    t = it * POOL_TM + lax.broadcasted_iota(jnp.int32, (POOL_TM, 1), 0)
    for gi, w in enumerate(POOL_WINDOWS):
        cols = slice(gi * POOL_GROUP, (gi + 1) * POOL_GROUP)
        s = ext[:, cols]
        span = 1
        while span < w:
            s = s + pltpu.roll(s, span, 0)
            span *= 2
        wsum = s[POOL_HALO:, :]
        count = jnp.minimum(t + 1, w).astype(F32)
        pooled = wsum / count - cur[:, cols]
        mixed = jnp.dot(pooled.astype(BF16), pw_ref[gi], preferred_element_type=F32)
        o_ref[:, cols] = (mixed * scale_ref[:, cols]).astype(o_ref.dtype)


def _pool(p, pool_w, pool_scale, layer):
    halo_blocks = POOL_TM // POOL_HALO
    return pl.pallas_call(
        _pool_kernel,
        grid=(TOKENS // POOL_TM,),
        in_specs=[
            pl.BlockSpec((POOL_TM, POOL_WIDTH), lambda i: (i, 0)),
            pl.BlockSpec((POOL_HALO, POOL_WIDTH), lambda i: (jnp.maximum(i * halo_blocks - 1, 0), 0)),
            pl.BlockSpec((None, len(POOL_WINDOWS), POOL_GROUP, POOL_GROUP), lambda i: (layer, 0, 0, 0)),
            pl.BlockSpec((None, 1, POOL_WIDTH), lambda i: (layer, 0, 0)),
        ],
        out_specs=pl.BlockSpec((POOL_TM, POOL_WIDTH), lambda i: (i, 0)),
        out_shape=jax.ShapeDtypeStruct((TOKENS, POOL_WIDTH), BF16),
        compiler_params=_params("parallel"),
        name="pool",
    )(p, p, pool_w, pool_scale)


def _attn_kernel(sinks_ref, q_ref, kc_ref, kp_ref, vc_ref, vp_ref, o_ref, *, layer):
    first = (pl.program_id(0) % (SEQ // ATTN_BLOCK)) == 0
    qi = lax.broadcasted_iota(jnp.int32, (ATTN_BLOCK, 2 * ATTN_BLOCK), 0)
    ki = lax.broadcasted_iota(jnp.int32, (ATTN_BLOCK, 2 * ATTN_BLOCK), 1)
    dist = qi + ATTN_BLOCK - ki
    first_key = jnp.where(first, ATTN_BLOCK, 0)
    valid = (dist >= 0) & (dist < ATTN_BLOCK) & (ki >= first_key)
    distf = dist.astype(F32)
    q = q_ref[...] * (ATTN_HEAD_DIM ** -0.5)
    kk = jnp.concatenate([kp_ref[...], kc_ref[...]], axis=0)
    vv = jnp.concatenate([vp_ref[...], vc_ref[...]], axis=0)
    for h in range(ATTN_HEADS):
        hk = h // ATTN_GROUP
        kv_cols = slice(hk * ATTN_HEAD_DIM, (hk + 1) * ATTN_HEAD_DIM)
        q_cols = slice(h * ATTN_HEAD_DIM, (h + 1) * ATTN_HEAD_DIM)
        slope = 2.0 ** (-8.0 * (h + 1) / ATTN_HEADS)
        sink = sinks_ref[layer, h]
        s = lax.dot_general(q[:, q_cols], kk[:, kv_cols], (((1,), (1,)), ((), ())),
                            preferred_element_type=F32)
        s = jnp.where(valid, s - slope * distf, -jnp.inf)
        m = jnp.maximum(jnp.max(s, axis=-1, keepdims=True), sink)
        e = jnp.exp(s - m)
        denom = jnp.sum(e, axis=-1, keepdims=True) + jnp.exp(sink - m)
        pv = jnp.dot(e.astype(BF16), vv[:, kv_cols], preferred_element_type=F32)
        o_ref[:, q_cols] = (pv / denom).astype(o_ref.dtype)


def _attn(p, sinks, layer):
    kcol = OFF_K // KV_WIDTH
    vcol = OFF_V // KV_WIDTH
    return pl.pallas_call(
        functools.partial(_attn_kernel, layer=layer),
        grid=(TOKENS // ATTN_BLOCK,),
        in_specs=[
            pl.BlockSpec(memory_space=pltpu.SMEM),
            pl.BlockSpec((ATTN_BLOCK, Q_WIDTH), lambda r: (r, OFF_Q // Q_WIDTH)),
            pl.BlockSpec((ATTN_BLOCK, KV_WIDTH), lambda r: (r, kcol)),
            pl.BlockSpec((ATTN_BLOCK, KV_WIDTH), lambda r: (jnp.maximum(r - 1, 0), kcol)),
            pl.BlockSpec((ATTN_BLOCK, KV_WIDTH), lambda r: (r, vcol)),
            pl.BlockSpec((ATTN_BLOCK, KV_WIDTH), lambda r: (jnp.maximum(r - 1, 0), vcol)),
        ],
        out_specs=pl.BlockSpec((ATTN_BLOCK, Q_WIDTH), lambda r: (r, 0)),
        out_shape=jax.ShapeDtypeStruct((TOKENS, Q_WIDTH), BF16),
        compiler_params=_params("parallel"),
        name="attn",
    )(sinks, p, p, p, p, p)


def _ret_kernel(logg_ref, q_ref, k_ref, v_ref, g_ref, norm_ref, o_ref, state_ref):
    c = pl.program_id(2)
    lg = logg_ref[pl.program_id(1)]

    @pl.when(c == 0)
    def _():
        state_ref[...] = jnp.zeros_like(state_ref)

    pos_r = lax.broadcasted_iota(jnp.int32, (RET_CHUNK, 1), 0).astype(F32)
    row = lax.broadcasted_iota(jnp.int32, (RET_CHUNK, RET_CHUNK), 0)
    col = lax.broadcasted_iota(jnp.int32, (RET_CHUNK, RET_CHUNK), 1)
    diff = (row - col).astype(F32)
    intra = jnp.where(diff >= 0, jnp.exp(lg * jnp.maximum(diff, 0.0)), 0.0)
    q_decay = jnp.exp(lg * (pos_r + 1.0))
    k_decay = jnp.exp(lg * (RET_CHUNK - 1.0 - pos_r))
    chunk_decay = jnp.exp(jnp.full((1, 1), RET_CHUNK, F32) * lg)

    q = q_ref[...]
    k = k_ref[...] * (RET_KEY_DIM ** -0.5)
    v = v_ref[...]
    state = state_ref[...]
    inner = lax.dot_general(q, k, (((1,), (1,)), ((), ())), preferred_element_type=F32) * intra
    qd = (q.astype(F32) * q_decay).astype(BF16)
    o = (jnp.dot(inner.astype(BF16), v, preferred_element_type=F32)
         + jnp.dot(qd, state.astype(BF16), preferred_element_type=F32))
    kd = (k.astype(F32) * k_decay).astype(BF16)
    state_ref[...] = state * chunk_decay + lax.dot_general(
        kd, v, (((0,), (0,)), ((), ())), preferred_element_type=F32)

    o = o * _rms_scale(o) * norm_ref[...]
    g = g_ref[...].astype(F32)
    o_ref[...] = (g * jax.nn.sigmoid(g) * o).astype(o_ref.dtype)


def _retention(p, log_g, ret_norm, layer):
    nc = SEQ // RET_CHUNK
    return pl.pallas_call(
        _ret_kernel,
        grid=(BATCH, RET_HEADS, nc),
        in_specs=[
            pl.BlockSpec(memory_space=pltpu.SMEM),
            pl.BlockSpec((RET_CHUNK, RET_KEY_DIM), lambda b, h, c: (b * nc + c, OFF_RQ // RET_KEY_DIM + h)),
            pl.BlockSpec((RET_CHUNK, RET_KEY_DIM), lambda b, h, c: (b * nc + c, OFF_RK // RET_KEY_DIM + h)),
            pl.BlockSpec((RET_CHUNK, RET_VALUE_DIM), lambda b, h, c: (b * nc + c, OFF_RV // RET_VALUE_DIM + h)),
            pl.BlockSpec((RET_CHUNK, RET_VALUE_DIM), lambda b, h, c: (b * nc + c, OFF_RG // RET_VALUE_DIM + h)),
            pl.BlockSpec((None, 1, RET_VALUE_DIM), lambda b, h, c: (layer, 0, h)),
        ],
        out_specs=pl.BlockSpec((RET_CHUNK, RET_VALUE_DIM), lambda b, h, c: (b * nc + c, h)),
        out_shape=jax.ShapeDtypeStruct((TOKENS, RV_WIDTH), BF16),
        scratch_shapes=[pltpu.VMEM((RET_KEY_DIM, RET_VALUE_DIM), F32)],
        compiler_params=_params("parallel", "parallel", "arbitrary"),
        name="retention",
    )(log_g, p, p, p, p, ret_norm)


def _merge_kernel(h_ref, pm_ref, at_ref, rt_ref, gp_ref, ga_ref, gr_ref,
                  wp_ref, wa_ref, wr_ref, wo_ref, post_ref, o_ref, acc_ref):
    j = pl.program_id(1)

    @pl.when(j == 0)
    def _():
        acc_ref[...] = jnp.zeros_like(acc_ref)

    def gate(ref):
        return jax.nn.sigmoid(ref[...].astype(F32))

    merged = (gate(gp_ref) * jnp.dot(pm_ref[...], wp_ref[...], preferred_element_type=F32)
              + gate(ga_ref) * jnp.dot(at_ref[...], wa_ref[...], preferred_element_type=F32)
              + gate(gr_ref) * jnp.dot(rt_ref[...], wr_ref[...], preferred_element_type=F32))
    acc_ref[...] += jnp.dot(merged.astype(BF16), wo_ref[...], preferred_element_type=F32)

    @pl.when(j == pl.num_programs(1) - 1)
    def _():
        y = acc_ref[...]
        o_ref[...] = h_ref[...] + y * _rms_scale(y) * post_ref[...]


def _merge(h, pm, at, rt, p, w_pool_out, w_attn_out, w_ret_out, w_out, post, layer):
    nj = D_MODEL // MERGE_TJ
    gcol = OFF_GATE // MERGE_TJ
    return pl.pallas_call(
        _merge_kernel,
        grid=(TOKENS // MERGE_TM, nj),
        in_specs=[
            pl.BlockSpec((MERGE_TM, D_MODEL), lambda i, j: (i, 0)),
            pl.BlockSpec((MERGE_TM, POOL_WIDTH), lambda i, j: (i, 0)),
            pl.BlockSpec((MERGE_TM, Q_WIDTH), lambda i, j: (i, 0)),
            pl.BlockSpec((MERGE_TM, RV_WIDTH), lambda i, j: (i, 0)),
            pl.BlockSpec((MERGE_TM, MERGE_TJ), lambda i, j: (i, gcol + j)),
            pl.BlockSpec((MERGE_TM, MERGE_TJ), lambda i, j: (i, gcol + nj + j)),
            pl.BlockSpec((MERGE_TM, MERGE_TJ), lambda i, j: (i, gcol + 2 * nj + j)),
            pl.BlockSpec((None, POOL_WIDTH, MERGE_TJ), lambda i, j: (layer, 0, j)),
            pl.BlockSpec((None, Q_WIDTH, MERGE_TJ), lambda i, j: (layer, 0, j)),
            pl.BlockSpec((None, RV_WIDTH, MERGE_TJ), lambda i, j: (layer, 0, j)),
            pl.BlockSpec((None, MERGE_TJ, D_MODEL), lambda i, j: (layer, j, 0)),
            pl.BlockSpec((None, 1, D_MODEL), lambda i, j: (layer, 0, 0)),
        ],
        out_specs=pl.BlockSpec((MERGE_TM, D_MODEL), lambda i, j: (i, 0)),
        out_shape=jax.ShapeDtypeStruct((TOKENS, D_MODEL), F32),
        scratch_shapes=[pltpu.VMEM((MERGE_TM, D_MODEL), F32)],
        compiler_params=_params("parallel", "arbitrary"),
        name="merge",
    )(h, pm, at, rt, p, p, p, w_pool_out, w_attn_out, w_ret_out, w_out, post)


def kernel(x, ffn1_pre, ffn1_up, ffn1_down, ffn1_post, mix_pre, w_in, pool_w, pool_scale, attn_sinks, ret_norm, w_pool_out, w_attn_out, w_ret_out, w_gate, w_out, mix_post, ffn2_pre, ffn2_up, ffn2_down, ffn2_post):
    def gain(a):
        return a.reshape(DEPTH, 1, a.shape[-1])

    def bf(a):
        return a.astype(BF16)

    w_cat = jnp.concatenate([bf(w_in), bf(w_gate)], axis=-1)
    ffn1_up, ffn1_down, ffn2_up, ffn2_down = bf(ffn1_up), bf(ffn1_down), bf(ffn2_up), bf(ffn2_down)
    pool_w, w_pool_out, w_attn_out, w_ret_out, w_out = (
        bf(pool_w), bf(w_pool_out), bf(w_attn_out), bf(w_ret_out), bf(w_out))
    log_g = jnp.log(1.0 - jnp.exp2(-5.0 - jnp.arange(RET_HEADS, dtype=F32)))

    h = x.reshape(TOKENS, D_MODEL)
    for l in range(DEPTH):
        h = _ffn(h, gain(ffn1_pre), ffn1_up, ffn1_down, gain(ffn1_post), l)
        p = _proj(h, gain(mix_pre), w_cat, l)
        pm = _pool(p, pool_w, gain(pool_scale), l)
        at = _attn(p, attn_sinks, l)
        rt = _retention(p, log_g, gain(ret_norm), l)
        h = _merge(h, pm, at, rt, p, w_pool_out, w_attn_out, w_ret_out, w_out, gain(mix_post), l)
        h = _ffn(h, gain(ffn2_pre), ffn2_up, ffn2_down, gain(ffn2_post), l)
    return h.reshape(BATCH, SEQ, D_MODEL)
```

```python
import functools

import jax
import jax.numpy as jnp
from jax import lax
from jax.experimental import pallas as pl
from jax.experimental.pallas import tpu as pltpu

F32 = jnp.float32
BF16 = jnp.bfloat16

D_MODEL = 2048
BATCH = 4
SEQ = 2048
DEPTH = 2
TOKENS = BATCH * SEQ

POOL_WINDOWS = (2, 4, 8, 16)
POOL_GROUP = 256
POOL_WIDTH = 1024
POOL_HALO = 16
ATTN_HEADS = 16
ATTN_KV_HEADS = 4
ATTN_GROUP = ATTN_HEADS // ATTN_KV_HEADS
ATTN_HEAD_DIM = 64
ATTN_BLOCK = 128
Q_WIDTH = 1024
KV_WIDTH = 256
RET_HEADS = 4
RET_KEY_DIM = 256
RET_VALUE_DIM = 512
RQK_WIDTH = 1024
RV_WIDTH = 2048
FFN_HIDDEN = 5632
RMS_EPS = 1e-6

OFF_XP = 0
OFF_Q = OFF_XP + POOL_WIDTH
OFF_K = OFF_Q + Q_WIDTH
OFF_V = OFF_K + KV_WIDTH
OFF_RQ = OFF_V + KV_WIDTH
OFF_RK = OFF_RQ + RQK_WIDTH
OFF_RV = OFF_RK + RQK_WIDTH
OFF_RG = OFF_RV + RV_WIDTH
IN_WIDTH = OFF_RG + RV_WIDTH
OFF_GATE = IN_WIDTH
PROJ_WIDTH = IN_WIDTH + 3 * D_MODEL

VMEM_LIMIT_BYTES = 56 * 1024 * 1024

FFN_TM = 512
FFN_TF = 512
PROJ_TM = 1024
PROJ_TN = 512
POOL_TM = 512
RET_CHUNK = 256
MERGE_TM = 512
MERGE_TJ = 512


def _params(*sem):
    return pltpu.CompilerParams(dimension_semantics=sem, vmem_limit_bytes=VMEM_LIMIT_BYTES)


def _rms_scale(x):
    return lax.rsqrt(jnp.mean(x * x, axis=-1, keepdims=True) + RMS_EPS)


def _ffn_kernel(h_ref, pre_ref, wg_ref, wu_ref, wd_ref, post_ref, o_ref, xn_ref, acc_ref):
    j = pl.program_id(1)

    @pl.when(j == 0)
    def _():
        x = h_ref[...]
        xn_ref[...] = (x * _rms_scale(x) * pre_ref[...]).astype(BF16)
        acc_ref[...] = jnp.zeros_like(acc_ref)

    xn = xn_ref[...]
    g = jnp.dot(xn, wg_ref[...], preferred_element_type=F32)
    u = jnp.dot(xn, wu_ref[...], preferred_element_type=F32)
    a = (g * jax.nn.sigmoid(g) * u).astype(BF16)
    acc_ref[...] += jnp.dot(a, wd_ref[...], preferred_element_type=F32)

    @pl.when(j == pl.num_programs(1) - 1)
    def _():
        y = acc_ref[...]
        o_ref[...] = h_ref[...] + 0.5 * (y * _rms_scale(y) * post_ref[...])


def _ffn(h, pre, w_up, w_down, post, layer):
    nf = FFN_HIDDEN // FFN_TF
    return pl.pallas_call(
        _ffn_kernel,
        grid=(TOKENS // FFN_TM, nf),
        in_specs=[
            pl.BlockSpec((FFN_TM, D_MODEL), lambda i, j: (i, 0)),
            pl.BlockSpec((None, 1, D_MODEL), lambda i, j: (layer, 0, 0)),
            pl.BlockSpec((None, D_MODEL, FFN_TF), lambda i, j: (layer, 0, j)),
            pl.BlockSpec((None, D_MODEL, FFN_TF), lambda i, j: (layer, 0, nf + j)),
            pl.BlockSpec((None, FFN_TF, D_MODEL), lambda i, j: (layer, j, 0)),
            pl.BlockSpec((None, 1, D_MODEL), lambda i, j: (layer, 0, 0)),
        ],
        out_specs=pl.BlockSpec((FFN_TM, D_MODEL), lambda i, j: (i, 0)),
        out_shape=jax.ShapeDtypeStruct((TOKENS, D_MODEL), F32),
        scratch_shapes=[pltpu.VMEM((FFN_TM, D_MODEL), BF16), pltpu.VMEM((FFN_TM, D_MODEL), F32)],
        compiler_params=_params("parallel", "arbitrary"),
        name="ffn",
    )(h, pre, w_up, w_up, w_down, post)


def _proj_kernel(h_ref, pre_ref, w_ref, o_ref, xn_ref):
    @pl.when(pl.program_id(1) == 0)
    def _():
        x = h_ref[...]
        xn_ref[...] = (x * _rms_scale(x) * pre_ref[...]).astype(BF16)

    o_ref[...] = jnp.dot(xn_ref[...], w_ref[...], preferred_element_type=F32).astype(o_ref.dtype)


def _proj(h, pre, w_cat, layer):
    return pl.pallas_call(
        _proj_kernel,
        grid=(TOKENS // PROJ_TM, PROJ_WIDTH // PROJ_TN),
        in_specs=[
            pl.BlockSpec((PROJ_TM, D_MODEL), lambda i, j: (i, 0)),
            pl.BlockSpec((None, 1, D_MODEL), lambda i, j: (layer, 0, 0)),
            pl.BlockSpec((None, D_MODEL, PROJ_TN), lambda i, j: (layer, 0, j)),
        ],
        out_specs=pl.BlockSpec((PROJ_TM, PROJ_TN), lambda i, j: (i, j)),
        out_shape=jax.ShapeDtypeStruct((TOKENS, PROJ_WIDTH), BF16),
        scratch_shapes=[pltpu.VMEM((PROJ_TM, D_MODEL), BF16)],
        compiler_params=_params("parallel", "arbitrary"),
        name="proj",
    )(h, pre, w_cat)


def _pool_kernel(xc_ref, xp_ref, pw_ref, scale_ref, o_ref):
    tiles_per_seq = SEQ // POOL_TM
    it = pl.program_id(0) % tiles_per_seq
    cur = xc_ref[...].astype(F32)
    prev = jnp.where(it == 0, 0.0, xp_ref[...].astype(F32))
    ext = jnp.concatenate([prev, cur], axis=0)
    t = it * POOL_TM + lax.broadcasted_iota(jnp.int32, (POOL_TM, 1), 0)
    for gi, w in enumerate(POOL_WINDOWS):
        cols = slice(gi * POOL_GROUP, (gi + 1) * POOL_GROUP)
        s = ext[:, cols]
        span = 1
        while span < w:
            s = s + pltpu.roll(s, span, 0)
            span *= 2
        wsum = s[POOL_HALO:, :]
        count = jnp.minimum(t + 1, w).astype(F32)
        pooled = wsum / count - cur[:, cols]
        mixed = jnp.dot(pooled.astype(BF16), pw_ref[gi], preferred_element_type=F32)
        o_ref[:, cols] = (mixed * scale_ref[:, cols]).astype(o_ref.dtype)


def _pool(p, pool_w, pool_scale, layer):
    halo_blocks = POOL_TM // POOL_HALO
    return pl.pallas_call(
        _pool_kernel,
        grid=(TOKENS // POOL_TM,),
        in_specs=[
            pl.BlockSpec((POOL_TM, POOL_WIDTH), lambda i: (i, 0)),
            pl.BlockSpec((POOL_HALO, POOL_WIDTH), lambda i: (jnp.maximum(i * halo_blocks - 1, 0), 0)),
            pl.BlockSpec((None, len(POOL_WINDOWS), POOL_GROUP, POOL_GROUP), lambda i: (layer, 0, 0, 0)),
            pl.BlockSpec((None, 1, POOL_WIDTH), lambda i: (layer, 0, 0)),
        ],
        out_specs=pl.BlockSpec((POOL_TM, POOL_WIDTH), lambda i: (i, 0)),
        out_shape=jax.ShapeDtypeStruct((TOKENS, POOL_WIDTH), BF16),
        compiler_params=_params("parallel"),
        name="pool",
    )(p, p, pool_w, pool_scale)


def _attn_kernel(sinks_ref, q_ref, kc_ref, kp_ref, vc_ref, vp_ref, o_ref, bias_ref, *, layer):
    qi = lax.broadcasted_iota(jnp.int32, (ATTN_BLOCK, ATTN_BLOCK), 0)
    ci = lax.broadcasted_iota(jnp.int32, (ATTN_BLOCK, ATTN_BLOCK), 1)
    from_prev = ci > qi

    @pl.when(pl.program_id(0) == 0)
    def _():
        dist = jnp.where(from_prev, qi + ATTN_BLOCK - ci, qi - ci).astype(F32)
        for h in range(ATTN_HEADS):
            alibi = -(2.0 ** (-8.0 * (h + 1) / ATTN_HEADS)) * dist
            bias_ref[0, h] = alibi
            bias_ref[1, h] = jnp.where(from_prev, -jnp.inf, alibi)

    seq_start = (pl.program_id(0) % (SEQ // ATTN_BLOCK) == 0).astype(jnp.int32)
    q = q_ref[...] * (ATTN_HEAD_DIM ** -0.5)
    kk = jnp.concatenate([kp_ref[...], kc_ref[...]], axis=0)
    vv = jnp.concatenate([vp_ref[...], vc_ref[...]], axis=0)
    for h in range(ATTN_HEADS):
        hk = h // ATTN_GROUP
        kv_cols = slice(hk * ATTN_HEAD_DIM, (hk + 1) * ATTN_HEAD_DIM)
        q_cols = slice(h * ATTN_HEAD_DIM, (h + 1) * ATTN_HEAD_DIM)
        sink = sinks_ref[layer, h]
        s2 = lax.dot_general(q[:, q_cols], kk[:, kv_cols], (((1,), (1,)), ((), ())),
                             preferred_element_type=F32)
        s = jnp.where(from_prev, s2[:, :ATTN_BLOCK], s2[:, ATTN_BLOCK:]) + bias_ref[seq_start, h]
        m = jnp.maximum(jnp.max(s, axis=-1, keepdims=True), sink)
        e = jnp.exp(s - m)
        denom = jnp.sum(e, axis=-1, keepdims=True) + jnp.exp(sink - m)
        eb = e.astype(BF16)
        zero = jnp.zeros_like(eb)
        e2 = jnp.concatenate([jnp.where(from_prev, eb, zero), jnp.where(from_prev, zero, eb)], axis=1)
        pv = jnp.dot(e2, vv[:, kv_cols], preferred_element_type=F32)
        o_ref[:, q_cols] = (pv / denom).astype(o_ref.dtype)


def _attn(p, sinks, layer):
    kcol = OFF_K // KV_WIDTH
    vcol = OFF_V // KV_WIDTH
    return pl.pallas_call(
        functools.partial(_attn_kernel, layer=layer),
        grid=(TOKENS // ATTN_BLOCK,),
        in_specs=[
            pl.BlockSpec(memory_space=pltpu.SMEM),
            pl.BlockSpec((ATTN_BLOCK, Q_WIDTH), lambda r: (r, OFF_Q // Q_WIDTH)),
            pl.BlockSpec((ATTN_BLOCK, KV_WIDTH), lambda r: (r, kcol)),
            pl.BlockSpec((ATTN_BLOCK, KV_WIDTH), lambda r: (jnp.maximum(r - 1, 0), kcol)),
            pl.BlockSpec((ATTN_BLOCK, KV_WIDTH), lambda r: (r, vcol)),
            pl.BlockSpec((ATTN_BLOCK, KV_WIDTH), lambda r: (jnp.maximum(r - 1, 0), vcol)),
        ],
        out_specs=pl.BlockSpec((ATTN_BLOCK, Q_WIDTH), lambda r: (r, 0)),
        out_shape=jax.ShapeDtypeStruct((TOKENS, Q_WIDTH), BF16),
        scratch_shapes=[pltpu.VMEM((2, ATTN_HEADS, ATTN_BLOCK, ATTN_BLOCK), F32)],
        compiler_params=_params("arbitrary"),
        name="attn",
    )(sinks, p, p, p, p, p)


def _ret_kernel(logg_ref, *refs):
    nh = RET_HEADS
    q_refs, k_refs, v_refs, g_refs = (refs[i * nh:(i + 1) * nh] for i in range(4))
    norm_ref, o_ref, state_ref, intra_ref, qdec_ref, kdec_ref = refs[4 * nh:]
    key_scale = RET_KEY_DIM ** -0.5

    @pl.when((pl.program_id(0) == 0) & (pl.program_id(1) == 0))
    def _():
        pos = lax.broadcasted_iota(jnp.int32, (RET_CHUNK, 1), 0).astype(F32)
        row = lax.broadcasted_iota(jnp.int32, (RET_CHUNK, RET_CHUNK), 0)
        col = lax.broadcasted_iota(jnp.int32, (RET_CHUNK, RET_CHUNK), 1)
        diff = (row - col).astype(F32)
        for h in range(nh):
            lg = logg_ref[h]
            intra_ref[h] = jnp.where(diff >= 0, jnp.exp(lg * jnp.maximum(diff, 0.0)), 0.0) * key_scale
            qdec_ref[h] = jnp.exp(lg * (pos + 1.0))
            kdec_ref[h] = jnp.exp(lg * (RET_CHUNK - 1.0 - pos)) * key_scale

    @pl.when(pl.program_id(1) == 0)
    def _():
        state_ref[...] = jnp.zeros_like(state_ref)

    for h in range(nh):
        v_cols = slice(h * RET_VALUE_DIM, (h + 1) * RET_VALUE_DIM)
        chunk_decay = jnp.exp(jnp.full((1, 1), RET_CHUNK, F32) * logg_ref[h])
        q = q_refs[h][...]
        k = k_refs[h][...]
        v = v_refs[h][...]
        state = state_ref[h]
        inner = lax.dot_general(q, k, (((1,), (1,)), ((), ())), preferred_element_type=F32) * intra_ref[h]
        qd = (q.astype(F32) * qdec_ref[h]).astype(BF16)
        o = (jnp.dot(inner.astype(BF16), v, preferred_element_type=F32)
             + jnp.dot(qd, state.astype(BF16), preferred_element_type=F32))
        kd = (k.astype(F32) * kdec_ref[h]).astype(BF16)
        state_ref[h] = state * chunk_decay + lax.dot_general(
            kd, v, (((0,), (0,)), ((), ())), preferred_element_type=F32)

        o = o * _rms_scale(o) * norm_ref[:, v_cols]
        g = g_refs[h][...].astype(F32)
        o_ref[:, v_cols] = (g * jax.nn.sigmoid(g) * o).astype(o_ref.dtype)


def _retention(p, log_g, ret_norm, layer):
    nc = SEQ // RET_CHUNK

    def head_specs(width, offset):
        return [pl.BlockSpec((RET_CHUNK, width), lambda b, c, h=h: (b * nc + c, offset // width + h))
                for h in range(RET_HEADS)]

    return pl.pallas_call(
        _ret_kernel,
        grid=(BATCH, nc),
        in_specs=[pl.BlockSpec(memory_space=pltpu.SMEM)]
        + head_specs(RET_KEY_DIM, OFF_RQ) + head_specs(RET_KEY_DIM, OFF_RK)
        + head_specs(RET_VALUE_DIM, OFF_RV) + head_specs(RET_VALUE_DIM, OFF_RG)
        + [pl.BlockSpec((None, 1, RV_WIDTH), lambda b, c: (layer, 0, 0))],
        out_specs=pl.BlockSpec((RET_CHUNK, RV_WIDTH), lambda b, c: (b * nc + c, 0)),
        out_shape=jax.ShapeDtypeStruct((TOKENS, RV_WIDTH), BF16),
        scratch_shapes=[
            pltpu.VMEM((RET_HEADS, RET_KEY_DIM, RET_VALUE_DIM), F32),
            pltpu.VMEM((RET_HEADS, RET_CHUNK, RET_CHUNK), F32),
            pltpu.VMEM((RET_HEADS, RET_CHUNK, 1), F32),
            pltpu.VMEM((RET_HEADS, RET_CHUNK, 1), F32),
        ],
        compiler_params=_params("arbitrary", "arbitrary"),
        name="retention",
    )(log_g, *([p] * (4 * RET_HEADS)), ret_norm)


def _merge_kernel(h_ref, pm_ref, at_ref, rt_ref, gp_ref, ga_ref, gr_ref,
                  wp_ref, wa_ref, wr_ref, wo_ref, post_ref, o_ref, acc_ref):
    j = pl.program_id(1)

    @pl.when(j == 0)
    def _():
        acc_ref[...] = jnp.zeros_like(acc_ref)

    def gate(ref):
        return jax.nn.sigmoid(ref[...].astype(F32))

    merged = (gate(gp_ref) * jnp.dot(pm_ref[...], wp_ref[...], preferred_element_type=F32)
              + gate(ga_ref) * jnp.dot(at_ref[...], wa_ref[...], preferred_element_type=F32)
              + gate(gr_ref) * jnp.dot(rt_ref[...], wr_ref[...], preferred_element_type=F32))
    acc_ref[...] += jnp.dot(merged.astype(BF16), wo_ref[...], preferred_element_type=F32)

    @pl.when(j == pl.num_programs(1) - 1)
    def _():
        y = acc_ref[...]
        o_ref[...] = h_ref[...] + y * _rms_scale(y) * post_ref[...]


def _merge(h, pm, at, rt, p, w_pool_out, w_attn_out, w_ret_out, w_out, post, layer):
    nj = D_MODEL // MERGE_TJ
    gcol = OFF_GATE // MERGE_TJ
    return pl.pallas_call(
        _merge_kernel,
        grid=(TOKENS // MERGE_TM, nj),
        in_specs=[
            pl.BlockSpec((MERGE_TM, D_MODEL), lambda i, j: (i, 0)),
            pl.BlockSpec((MERGE_TM, POOL_WIDTH), lambda i, j: (i, 0)),
            pl.BlockSpec((MERGE_TM, Q_WIDTH), lambda i, j: (i, 0)),
            pl.BlockSpec((MERGE_TM, RV_WIDTH), lambda i, j: (i, 0)),
            pl.BlockSpec((MERGE_TM, MERGE_TJ), lambda i, j: (i, gcol + j)),
            pl.BlockSpec((MERGE_TM, MERGE_TJ), lambda i, j: (i, gcol + nj + j)),
            pl.BlockSpec((MERGE_TM, MERGE_TJ), lambda i, j: (i, gcol + 2 * nj + j)),
            pl.BlockSpec((None, POOL_WIDTH, MERGE_TJ), lambda i, j: (layer, 0, j)),
            pl.BlockSpec((None, Q_WIDTH, MERGE_TJ), lambda i, j: (layer, 0, j)),
            pl.BlockSpec((None, RV_WIDTH, MERGE_TJ), lambda i, j: (layer, 0, j)),
            pl.BlockSpec((None, MERGE_TJ, D_MODEL), lambda i, j: (layer, j, 0)),
            pl.BlockSpec((None, 1, D_MODEL), lambda i, j: (layer, 0, 0)),
        ],
        out_specs=pl.BlockSpec((MERGE_TM, D_MODEL), lambda i, j: (i, 0)),
        out_shape=jax.ShapeDtypeStruct((TOKENS, D_MODEL), F32),
        scratch_shapes=[pltpu.VMEM((MERGE_TM, D_MODEL), F32)],
        compiler_params=_params("parallel", "arbitrary"),
        name="merge",
    )(h, pm, at, rt, p, p, p, w_pool_out, w_attn_out, w_ret_out, w_out, post)


def kernel(x, ffn1_pre, ffn1_up, ffn1_down, ffn1_post, mix_pre, w_in, pool_w, pool_scale, attn_sinks, ret_norm, w_pool_out, w_attn_out, w_ret_out, w_gate, w_out, mix_post, ffn2_pre, ffn2_up, ffn2_down, ffn2_post):
    def gain(a):
        return a.reshape(DEPTH, 1, a.shape[-1])

    def bf(a):
        return a.astype(BF16)

    w_cat = jnp.concatenate([bf(w_in), bf(w_gate)], axis=-1)
    ffn1_up, ffn1_down, ffn2_up, ffn2_down = bf(ffn1_up), bf(ffn1_down), bf(ffn2_up), bf(ffn2_down)
    pool_w, w_pool_out, w_attn_out, w_ret_out, w_out = (
        bf(pool_w), bf(w_pool_out), bf(w_attn_out), bf(w_ret_out), bf(w_out))
    log_g = jnp.log(1.0 - jnp.exp2(-5.0 - jnp.arange(RET_HEADS, dtype=F32)))

    h = x.reshape(TOKENS, D_MODEL)
    for l in range(DEPTH):
        h = _ffn(h, gain(ffn1_pre), ffn1_up, ffn1_down, gain(ffn1_post), l)
        p = _proj(h, gain(mix_pre), w_cat, l)
        pm = _pool(p, pool_w, gain(pool_scale), l)
        at = _attn(p, attn_sinks, l)
        rt = _retention(p, log_g, gain(ret_norm), l)
        h = _merge(h, pm, at, rt, p, w_pool_out, w_attn_out, w_ret_out, w_out, gain(mix_post), l)
        h = _ffn(h, gain(ffn2_pre), ffn2_up, ffn2_down, gain(ffn2_post), l)
    return h.reshape(BATCH, SEQ, D_MODEL)
```

```python
import functools

import jax
import jax.numpy as jnp
from jax import lax
from jax.experimental import pallas as pl
from jax.experimental.pallas import tpu as pltpu

F32 = jnp.float32
BF16 = jnp.bfloat16

D_MODEL = 2048
BATCH = 4
SEQ = 2048
DEPTH = 2
TOKENS = BATCH * SEQ

POOL_WINDOWS = (2, 4, 8, 16)
POOL_GROUP = 256
POOL_WIDTH = 1024
POOL_HALO = 16
ATTN_HEADS = 16
ATTN_KV_HEADS = 4
ATTN_GROUP = ATTN_HEADS // ATTN_KV_HEADS
ATTN_HEAD_DIM = 64
ATTN_BLOCK = 128
Q_WIDTH = 1024
KV_WIDTH = 256
RET_HEADS = 4
RET_KEY_DIM = 256
RET_VALUE_DIM = 512
RQK_WIDTH = 1024
RV_WIDTH = 2048
FFN_HIDDEN = 5632
RMS_EPS = 1e-6

OFF_XP = 0
OFF_Q = OFF_XP + POOL_WIDTH
OFF_K = OFF_Q + Q_WIDTH
OFF_V = OFF_K + KV_WIDTH
OFF_RQ = OFF_V + KV_WIDTH
OFF_RK = OFF_RQ + RQK_WIDTH
OFF_RV = OFF_RK + RQK_WIDTH
OFF_RG = OFF_RV + RV_WIDTH
IN_WIDTH = OFF_RG + RV_WIDTH
OFF_GATE = IN_WIDTH
PROJ_WIDTH = IN_WIDTH + 3 * D_MODEL

VMEM_LIMIT_BYTES = 56 * 1024 * 1024

FFN_TM = 1024
FFN_TF = 256
PROJ_TM = 1024
PROJ_TN = 512
POOL_TM = 512
RET_CHUNK = 256
MERGE_TM = 512
MERGE_TJ = 512


def _params(*sem):
    return pltpu.CompilerParams(dimension_semantics=sem, vmem_limit_bytes=VMEM_LIMIT_BYTES)


def _rms_scale(x):
    return lax.rsqrt(jnp.mean(x * x, axis=-1, keepdims=True) + RMS_EPS)


def _ffn_kernel(h_ref, pre_ref, wg_ref, wu_ref, wd_ref, post_ref, o_ref, xn_ref):
    j = pl.program_id(1)

    @pl.when(j == 0)
    def _():
        x = h_ref[...]
        xn_ref[...] = (x * _rms_scale(x) * pre_ref[...]).astype(BF16)
        o_ref[...] = jnp.zeros_like(o_ref)

    w_gu = jnp.concatenate([wg_ref[...].astype(BF16), wu_ref[...].astype(BF16)], axis=1)
    gu = jnp.dot(xn_ref[...], w_gu, preferred_element_type=F32)
    g = gu[:, :FFN_TF]
    a = (g * jax.nn.sigmoid(g) * gu[:, FFN_TF:]).astype(BF16)
    o_ref[...] += jnp.dot(a, wd_ref[...].astype(BF16), preferred_element_type=F32)

    @pl.when(j == pl.num_programs(1) - 1)
    def _():
        y = o_ref[...]
        o_ref[...] = h_ref[...] + 0.5 * (y * _rms_scale(y) * post_ref[...])


def _ffn(h, pre, w_up, w_down, post, layer):
    nf = FFN_HIDDEN // FFN_TF
    return pl.pallas_call(
        _ffn_kernel,
        grid=(TOKENS // FFN_TM, nf),
        in_specs=[
            pl.BlockSpec((FFN_TM, D_MODEL), lambda i, j: (i, 0), pipeline_mode=pl.Buffered(1)),
            pl.BlockSpec((None, 1, D_MODEL), lambda i, j: (layer, 0, 0)),
            pl.BlockSpec((None, D_MODEL, FFN_TF), lambda i, j: (layer, 0, j)),
            pl.BlockSpec((None, D_MODEL, FFN_TF), lambda i, j: (layer, 0, nf + j)),
            pl.BlockSpec((None, FFN_TF, D_MODEL), lambda i, j: (layer, j, 0)),
            pl.BlockSpec((None, 1, D_MODEL), lambda i, j: (layer, 0, 0)),
        ],
        out_specs=pl.BlockSpec((FFN_TM, D_MODEL), lambda i, j: (i, 0)),
        out_shape=jax.ShapeDtypeStruct((TOKENS, D_MODEL), F32),
        scratch_shapes=[pltpu.VMEM((FFN_TM, D_MODEL), BF16)],
        compiler_params=_params("parallel", "arbitrary"),
        name="ffn",
    )(h, pre, w_up, w_up, w_down, post)


def _proj_kernel(h_ref, pre_ref, win_ref, wgate_ref, o_ref, xn_ref):
    j = pl.program_id(1)
    n_in = IN_WIDTH // PROJ_TN

    @pl.when(j == 0)
    def _():
        x = h_ref[...]
        xn_ref[...] = (x * _rms_scale(x) * pre_ref[...]).astype(BF16)

    def project(w_ref):
        o_ref[...] = jnp.dot(xn_ref[...], w_ref[...].astype(BF16),
                             preferred_element_type=F32).astype(o_ref.dtype)

    pl.when(j < n_in)(functools.partial(project, win_ref))
    pl.when(j >= n_in)(functools.partial(project, wgate_ref))


def _proj(h, pre, w_in, w_gate, layer):
    n_in = IN_WIDTH // PROJ_TN
    return pl.pallas_call(
        _proj_kernel,
        grid=(TOKENS // PROJ_TM, PROJ_WIDTH // PROJ_TN),
        in_specs=[
            pl.BlockSpec((PROJ_TM, D_MODEL), lambda i, j: (i, 0)),
            pl.BlockSpec((None, 1, D_MODEL), lambda i, j: (layer, 0, 0)),
            pl.BlockSpec((None, D_MODEL, PROJ_TN), lambda i, j: (layer, 0, jnp.minimum(j, n_in - 1))),
            pl.BlockSpec((None, D_MODEL, PROJ_TN), lambda i, j: (layer, 0, jnp.maximum(j - n_in, 0))),
        ],
        out_specs=pl.BlockSpec((PROJ_TM, PROJ_TN), lambda i, j: (i, j)),
        out_shape=jax.ShapeDtypeStruct((TOKENS, PROJ_WIDTH), BF16),
        scratch_shapes=[pltpu.VMEM((PROJ_TM, D_MODEL), BF16)],
        compiler_params=_params("parallel", "arbitrary"),
        name="proj",
    )(h, pre, w_in, w_gate)


def _pool_kernel(xc_ref, xp_ref, pw_ref, scale_ref, o_ref):
    tiles_per_seq = SEQ // POOL_TM
    it = pl.program_id(0) % tiles_per_seq
    cur = xc_ref[...].astype(F32)
    prev = jnp.where(it == 0, 0.0, xp_ref[...].astype(F32))
    ext = jnp.concatenate([prev, cur], axis=0)
    t = it * POOL_TM + lax.broadcasted_iota(jnp.int32, (POOL_TM, 1), 0)
    for gi, w in enumerate(POOL_WINDOWS):
        cols = slice(gi * POOL_GROUP, (gi + 1) * POOL_GROUP)
        s = ext[:, cols]
        span = 1
        while span < w:
            s = s + pltpu.roll(s, span, 0)
            span *= 2
        wsum = s[POOL_HALO:, :]
        count = jnp.minimum(t + 1, w).astype(F32)
        pooled = wsum / count - cur[:, cols]
        mixed = jnp.dot(pooled.astype(BF16), pw_ref[gi], preferred_element_type=F32)
        o_ref[:, cols] = (mixed * scale_ref[:, cols]).astype(o_ref.dtype)


def _pool(p, pool_w, pool_scale, layer):
    halo_blocks = POOL_TM // POOL_HALO
    return pl.pallas_call(
        _pool_kernel,
        grid=(TOKENS // POOL_TM,),
        in_specs=[
            pl.BlockSpec((POOL_TM, POOL_WIDTH), lambda i: (i, 0)),
            pl.BlockSpec((POOL_HALO, POOL_WIDTH), lambda i: (jnp.maximum(i * halo_blocks - 1, 0), 0)),
            pl.BlockSpec((None, len(POOL_WINDOWS), POOL_GROUP, POOL_GROUP), lambda i: (layer, 0, 0, 0)),
            pl.BlockSpec((None, 1, POOL_WIDTH), lambda i: (layer, 0, 0)),
        ],
        out_specs=pl.BlockSpec((POOL_TM, POOL_WIDTH), lambda i: (i, 0)),
        out_shape=jax.ShapeDtypeStruct((TOKENS, POOL_WIDTH), BF16),
        compiler_params=_params("parallel"),
        name="pool",
    )(p, p, pool_w, pool_scale)


def _attn_kernel(sinks_ref, q_ref, kc_ref, kp_ref, vc_ref, vp_ref, o_ref, bias_ref, *, layer):
    qi = lax.broadcasted_iota(jnp.int32, (ATTN_BLOCK, ATTN_BLOCK), 0)
    ci = lax.broadcasted_iota(jnp.int32, (ATTN_BLOCK, ATTN_BLOCK), 1)
    from_prev = ci > qi

    @pl.when(pl.program_id(0) == 0)
    def _():
        dist = jnp.where(from_prev, qi + ATTN_BLOCK - ci, qi - ci).astype(F32)
        for h in range(ATTN_HEADS):
            alibi = -(2.0 ** (-8.0 * (h + 1) / ATTN_HEADS)) * dist
            bias_ref[0, h] = alibi
            bias_ref[1, h] = jnp.where(from_prev, -jnp.inf, alibi)

    seq_start = (pl.program_id(0) % (SEQ // ATTN_BLOCK) == 0).astype(jnp.int32)
    q = q_ref[...] * (ATTN_HEAD_DIM ** -0.5)
    kk = jnp.concatenate([kp_ref[...], kc_ref[...]], axis=0)
    vv = jnp.concatenate([vp_ref[...], vc_ref[...]], axis=0)
    for h in range(ATTN_HEADS):
        hk = h // ATTN_GROUP
        kv_cols = slice(hk * ATTN_HEAD_DIM, (hk + 1) * ATTN_HEAD_DIM)
        q_cols = slice(h * ATTN_HEAD_DIM, (h + 1) * ATTN_HEAD_DIM)
        sink = sinks_ref[layer, h]
        s2 = lax.dot_general(q[:, q_cols], kk[:, kv_cols], (((1,), (1,)), ((), ())),
                             preferred_element_type=F32)
        s = jnp.where(from_prev, s2[:, :ATTN_BLOCK], s2[:, ATTN_BLOCK:]) + bias_ref[seq_start, h]
        m = jnp.maximum(jnp.max(s, axis=-1, keepdims=True), sink)
        e = jnp.exp(s - m)
        denom = jnp.sum(e, axis=-1, keepdims=True) + jnp.exp(sink - m)
        eb = e.astype(BF16)
        zero = jnp.zeros_like(eb)
        e2 = jnp.concatenate([jnp.where(from_prev, eb, zero), jnp.where(from_prev, zero, eb)], axis=1)
        pv = jnp.dot(e2, vv[:, kv_cols], preferred_element_type=F32)
        o_ref[:, q_cols] = (pv / denom).astype(o_ref.dtype)


def _attn(p, sinks, layer):
    kcol = OFF_K // KV_WIDTH
    vcol = OFF_V // KV_WIDTH
    return pl.pallas_call(
        functools.partial(_attn_kernel, layer=layer),
        grid=(TOKENS // ATTN_BLOCK,),
        in_specs=[
            pl.BlockSpec(memory_space=pltpu.SMEM),
            pl.BlockSpec((ATTN_BLOCK, Q_WIDTH), lambda r: (r, OFF_Q // Q_WIDTH)),
            pl.BlockSpec((ATTN_BLOCK, KV_WIDTH), lambda r: (r, kcol)),
            pl.BlockSpec((ATTN_BLOCK, KV_WIDTH), lambda r: (jnp.maximum(r - 1, 0), kcol)),
            pl.BlockSpec((ATTN_BLOCK, KV_WIDTH), lambda r: (r, vcol)),
            pl.BlockSpec((ATTN_BLOCK, KV_WIDTH), lambda r: (jnp.maximum(r - 1, 0), vcol)),
        ],
        out_specs=pl.BlockSpec((ATTN_BLOCK, Q_WIDTH), lambda r: (r, 0)),
        out_shape=jax.ShapeDtypeStruct((TOKENS, Q_WIDTH), BF16),
        scratch_shapes=[pltpu.VMEM((2, ATTN_HEADS, ATTN_BLOCK, ATTN_BLOCK), F32)],
        compiler_params=_params("arbitrary"),
        name="attn",
    )(sinks, p, p, p, p, p)


def _ret_kernel(logg_ref, *refs):
    nh = RET_HEADS
    q_refs, k_refs, v_refs, g_refs = (refs[i * nh:(i + 1) * nh] for i in range(4))
    norm_ref, o_ref, state_ref, intra_ref, qdec_ref, kdec_ref = refs[4 * nh:]
    key_scale = RET_KEY_DIM ** -0.5

    @pl.when((pl.program_id(0) == 0) & (pl.program_id(1) == 0))
    def _():
        pos = lax.broadcasted_iota(jnp.int32, (RET_CHUNK, 1), 0).astype(F32)
        row = lax.broadcasted_iota(jnp.int32, (RET_CHUNK, RET_CHUNK), 0)
        col = lax.broadcasted_iota(jnp.int32, (RET_CHUNK, RET_CHUNK), 1)
        diff = (row - col).astype(F32)
        for h in range(nh):
            lg = logg_ref[h]
            intra_ref[h] = jnp.where(diff >= 0, jnp.exp(lg * jnp.maximum(diff, 0.0)), 0.0) * key_scale
            qdec_ref[h] = jnp.exp(lg * (pos + 1.0))
            kdec_ref[h] = jnp.exp(lg * (RET_CHUNK - 1.0 - pos)) * key_scale

    @pl.when(pl.program_id(1) == 0)
    def _():
        state_ref[...] = jnp.zeros_like(state_ref)

    for h in range(nh):
        v_cols = slice(h * RET_VALUE_DIM, (h + 1) * RET_VALUE_DIM)
        chunk_decay = jnp.exp(jnp.full((1, 1), RET_CHUNK, F32) * logg_ref[h])
        q = q_refs[h][...]
        k = k_refs[h][...]
        v = v_refs[h][...]
        state = state_ref[h]
        inner = lax.dot_general(q, k, (((1,), (1,)), ((), ())), preferred_element_type=F32) * intra_ref[h]
        qd = (q.astype(F32) * qdec_ref[h]).astype(BF16)
        o = (jnp.dot(inner.astype(BF16), v, preferred_element_type=F32)
             + jnp.dot(qd, state.astype(BF16), preferred_element_type=F32))
        kd = (k.astype(F32) * kdec_ref[h]).astype(BF16)
        state_ref[h] = state * chunk_decay + lax.dot_general(
            kd, v, (((0,), (0,)), ((), ())), preferred_element_type=F32)

        o = o * _rms_scale(o) * norm_ref[:, v_cols]
        g = g_refs[h][...].astype(F32)
        o_ref[:, v_cols] = (g * jax.nn.sigmoid(g) * o).astype(o_ref.dtype)


def _retention(p, log_g, ret_norm, layer):
    nc = SEQ // RET_CHUNK

    def head_specs(width, offset):
        return [pl.BlockSpec((RET_CHUNK, width), lambda b, c, h=h: (b * nc + c, offset // width + h))
                for h in range(RET_HEADS)]

    return pl.pallas_call(
        _ret_kernel,
        grid=(BATCH, nc),
        in_specs=[pl.BlockSpec(memory_space=pltpu.SMEM)]
        + head_specs(RET_KEY_DIM, OFF_RQ) + head_specs(RET_KEY_DIM, OFF_RK)
        + head_specs(RET_VALUE_DIM, OFF_RV) + head_specs(RET_VALUE_DIM, OFF_RG)
        + [pl.BlockSpec((None, 1, RV_WIDTH), lambda b, c: (layer, 0, 0))],
        out_specs=pl.BlockSpec((RET_CHUNK, RV_WIDTH), lambda b, c: (b * nc + c, 0)),
        out_shape=jax.ShapeDtypeStruct((TOKENS, RV_WIDTH), BF16),
        scratch_shapes=[
            pltpu.VMEM((RET_HEADS, RET_KEY_DIM, RET_VALUE_DIM), F32),
            pltpu.VMEM((RET_HEADS, RET_CHUNK, RET_CHUNK), F32),
            pltpu.VMEM((RET_HEADS, RET_CHUNK, 1), F32),
            pltpu.VMEM((RET_HEADS, RET_CHUNK, 1), F32),
        ],
        compiler_params=_params("arbitrary", "arbitrary"),
        name="retention",
    )(log_g, *([p] * (4 * RET_HEADS)), ret_norm)


def _merge_kernel(h_ref, pm_ref, at_ref, rt_ref, gp_ref, ga_ref, gr_ref,
                  wp_ref, wa_ref, wr_ref, wo_ref, post_ref, o_ref, acc_ref):
    j = pl.program_id(1)

    @pl.when(j == 0)
    def _():
        acc_ref[...] = jnp.zeros_like(acc_ref)

    def gate(ref):
        return jax.nn.sigmoid(ref[...].astype(F32))

    merged = (gate(gp_ref) * jnp.dot(pm_ref[...], wp_ref[...], preferred_element_type=F32)
              + gate(ga_ref) * jnp.dot(at_ref[...], wa_ref[...], preferred_element_type=F32)
              + gate(gr_ref) * jnp.dot(rt_ref[...], wr_ref[...], preferred_element_type=F32))
    acc_ref[...] += jnp.dot(merged.astype(BF16), wo_ref[...], preferred_element_type=F32)

    @pl.when(j == pl.num_programs(1) - 1)
    def _():
        y = acc_ref[...]
        o_ref[...] = h_ref[...] + y * _rms_scale(y) * post_ref[...]


def _merge(h, pm, at, rt, p, w_pool_out, w_attn_out, w_ret_out, w_out, post, layer):
    nj = D_MODEL // MERGE_TJ
    gcol = OFF_GATE // MERGE_TJ
    return pl.pallas_call(
        _merge_kernel,
        grid=(TOKENS // MERGE_TM, nj),
        in_specs=[
            pl.BlockSpec((MERGE_TM, D_MODEL), lambda i, j: (i, 0)),
            pl.BlockSpec((MERGE_TM, POOL_WIDTH), lambda i, j: (i, 0)),
            pl.BlockSpec((MERGE_TM, Q_WIDTH), lambda i, j: (i, 0)),
            pl.BlockSpec((MERGE_TM, RV_WIDTH), lambda i, j: (i, 0)),
            pl.BlockSpec((MERGE_TM, MERGE_TJ), lambda i, j: (i, gcol + j)),
            pl.BlockSpec((MERGE_TM, MERGE_TJ), lambda i, j: (i, gcol + nj + j)),
            pl.BlockSpec((MERGE_TM, MERGE_TJ), lambda i, j: (i, gcol + 2 * nj + j)),
            pl.BlockSpec((None, POOL_WIDTH, MERGE_TJ), lambda i, j: (layer, 0, j)),
            pl.BlockSpec((None, Q_WIDTH, MERGE_TJ), lambda i, j: (layer, 0, j)),
            pl.BlockSpec((None, RV_WIDTH, MERGE_TJ), lambda i, j: (layer, 0, j)),
            pl.BlockSpec((None, MERGE_TJ, D_MODEL), lambda i, j: (layer, j, 0)),
            pl.BlockSpec((None, 1, D_MODEL), lambda i, j: (layer, 0, 0)),
        ],
        out_specs=pl.BlockSpec((MERGE_TM, D_MODEL), lambda i, j: (i, 0)),
        out_shape=jax.ShapeDtypeStruct((TOKENS, D_MODEL), F32),
        scratch_shapes=[pltpu.VMEM((MERGE_TM, D_MODEL), F32)],
        compiler_params=_params("parallel", "arbitrary"),
        name="merge",
    )(h, pm, at, rt, p, p, p, w_pool_out, w_attn_out, w_ret_out, w_out, post)


def kernel(x, ffn1_pre, ffn1_up, ffn1_down, ffn1_post, mix_pre, w_in, pool_w, pool_scale, attn_sinks, ret_norm, w_pool_out, w_attn_out, w_ret_out, w_gate, w_out, mix_post, ffn2_pre, ffn2_up, ffn2_down, ffn2_post):
    def gain(a):
        return a.reshape(DEPTH, 1, a.shape[-1])

    def bf(a):
        return a.astype(BF16)

    pool_w, w_pool_out, w_attn_out, w_ret_out, w_out = (
        bf(pool_w), bf(w_pool_out), bf(w_attn_out), bf(w_ret_out), bf(w_out))
    log_g = jnp.log(1.0 - jnp.exp2(-5.0 - jnp.arange(RET_HEADS, dtype=F32)))

    h = x.reshape(TOKENS, D_MODEL)
    for l in range(DEPTH):
        h = _ffn(h, gain(ffn1_pre), ffn1_up, ffn1_down, gain(ffn1_post), l)
        p = _proj(h, gain(mix_pre), w_in, w_gate, l)
        pm = _pool(p, pool_w, gain(pool_scale), l)
        at = _attn(p, attn_sinks, l)
        rt = _retention(p, log_g, gain(ret_norm), l)
        h = _merge(h, pm, at, rt, p, w_pool_out, w_attn_out, w_ret_out, w_out, gain(mix_post), l)
        h = _ffn(h, gain(ffn2_pre), ffn2_up, ffn2_down, gain(ffn2_post), l)
    return h.reshape(BATCH, SEQ, D_MODEL)
```

```python
import functools

import jax
import jax.numpy as jnp
from jax import lax
from jax.experimental import pallas as pl
from jax.experimental.pallas import tpu as pltpu

F32 = jnp.float32
BF16 = jnp.bfloat16

D_MODEL = 2048
BATCH = 4
SEQ = 2048
DEPTH = 2
TOKENS = BATCH * SEQ

POOL_WINDOWS = (2, 4, 8, 16)
POOL_GROUP = 256
POOL_WIDTH = 1024
POOL_HALO = 16
ATTN_HEADS = 16
ATTN_KV_HEADS = 4
ATTN_GROUP = ATTN_HEADS // ATTN_KV_HEADS
ATTN_HEAD_DIM = 64
ATTN_BLOCK = 128
Q_WIDTH = 1024
KV_WIDTH = 256
RET_HEADS = 4
RET_KEY_DIM = 256
RET_VALUE_DIM = 512
RQK_WIDTH = 1024
RV_WIDTH = 2048
FFN_HIDDEN = 5632
RMS_EPS = 1e-6

OFF_XP = 0
OFF_Q = OFF_XP + POOL_WIDTH
OFF_K = OFF_Q + Q_WIDTH
OFF_V = OFF_K + KV_WIDTH
OFF_RQ = OFF_V + KV_WIDTH
OFF_RK = OFF_RQ + RQK_WIDTH
OFF_RV = OFF_RK + RQK_WIDTH
OFF_RG = OFF_RV + RV_WIDTH
IN_WIDTH = OFF_RG + RV_WIDTH
OFF_GATE = IN_WIDTH
PROJ_WIDTH = IN_WIDTH + 3 * D_MODEL

VMEM_LIMIT_BYTES = 60 * 1024 * 1024

FFN_TM = 1024
FFN_TF = 256
PROJ_TM = 2048
PROJ_TN = 512
POOL_TM = 512
RET_CHUNK = 256
MERGE_TM = 512
MERGE_TJ = 512


def _params(*sem):
    return pltpu.CompilerParams(dimension_semantics=sem, vmem_limit_bytes=VMEM_LIMIT_BYTES)


def _rms_scale(x):
    return lax.rsqrt(jnp.mean(x * x, axis=-1, keepdims=True) + RMS_EPS)


def _ffn_kernel(h_ref, pre_ref, wg_ref, wu_ref, wd_ref, post_ref, o_ref, xn_ref, scale_ref):
    j = pl.program_id(1)

    @pl.when(j == 0)
    def _():
        x = h_ref[...]
        xn_ref[...] = (x * _rms_scale(x) * pre_ref[...]).astype(BF16)
        o_ref[...] = jnp.zeros_like(o_ref)

    w_gu = jnp.concatenate([wg_ref[...].astype(BF16), wu_ref[...].astype(BF16)], axis=1)
    gu = jnp.dot(xn_ref[...], w_gu, preferred_element_type=F32)
    g = gu[:, :FFN_TF]
    a = (g * jax.nn.sigmoid(g) * gu[:, FFN_TF:]).astype(BF16)
    o_ref[...] += jnp.dot(a, wd_ref[...].astype(BF16), preferred_element_type=F32)

    @pl.when(j == pl.num_programs(1) - 1)
    def _():
        scale_ref[...] = _rms_scale(o_ref[...])
        o_ref[...] = h_ref[...] + 0.5 * (o_ref[...] * scale_ref[...] * post_ref[...])


def _ffn(h, pre, w_up, w_down, post, layer):
    nf = FFN_HIDDEN // FFN_TF
    return pl.pallas_call(
        _ffn_kernel,
        grid=(TOKENS // FFN_TM, nf),
        in_specs=[
            pl.BlockSpec((FFN_TM, D_MODEL), lambda i, j: (i, 0)),
            pl.BlockSpec((None, 1, D_MODEL), lambda i, j: (layer, 0, 0)),
            pl.BlockSpec((None, D_MODEL, FFN_TF), lambda i, j: (layer, 0, j)),
            pl.BlockSpec((None, D_MODEL, FFN_TF), lambda i, j: (layer, 0, nf + j)),
            pl.BlockSpec((None, FFN_TF, D_MODEL), lambda i, j: (layer, j, 0)),
            pl.BlockSpec((None, 1, D_MODEL), lambda i, j: (layer, 0, 0)),
        ],
        out_specs=pl.BlockSpec((FFN_TM, D_MODEL), lambda i, j: (i, 0)),
        out_shape=jax.ShapeDtypeStruct((TOKENS, D_MODEL), F32),
        scratch_shapes=[pltpu.VMEM((FFN_TM, D_MODEL), BF16), pltpu.VMEM((FFN_TM, 1), F32)],
        compiler_params=_params("parallel", "arbitrary"),
        name="ffn",
    )(h, pre, w_up, w_up, w_down, post)


def _proj_kernel(h_ref, pre_ref, win_ref, wgate_ref, o_ref, xn_ref):
    j = pl.program_id(1)
    n_in = IN_WIDTH // PROJ_TN

    @pl.when(j == 0)
    def _():
        x = h_ref[...]
        xn_ref[...] = (x * _rms_scale(x) * pre_ref[...]).astype(BF16)

    def project(w_ref):
        o_ref[...] = jnp.dot(xn_ref[...], w_ref[...].astype(BF16),
                             preferred_element_type=F32).astype(o_ref.dtype)

    pl.when(j < n_in)(functools.partial(project, win_ref))
    pl.when(j >= n_in)(functools.partial(project, wgate_ref))


def _proj(h, pre, w_in, w_gate, layer):
    n_in = IN_WIDTH // PROJ_TN
    return pl.pallas_call(
        _proj_kernel,
        grid=(TOKENS // PROJ_TM, PROJ_WIDTH // PROJ_TN),
        in_specs=[
            pl.BlockSpec((PROJ_TM, D_MODEL), lambda i, j: (i, 0), pipeline_mode=pl.Buffered(1)),
            pl.BlockSpec((None, 1, D_MODEL), lambda i, j: (layer, 0, 0)),
            pl.BlockSpec((None, D_MODEL, PROJ_TN), lambda i, j: (layer, 0, jnp.minimum(j, n_in - 1))),
            pl.BlockSpec((None, D_MODEL, PROJ_TN), lambda i, j: (layer, 0, jnp.maximum(j - n_in, 0))),
        ],
        out_specs=pl.BlockSpec((PROJ_TM, PROJ_TN), lambda i, j: (i, j)),
        out_shape=jax.ShapeDtypeStruct((TOKENS, PROJ_WIDTH), BF16),
        scratch_shapes=[pltpu.VMEM((PROJ_TM, D_MODEL), BF16)],
        compiler_params=_params("parallel", "arbitrary"),
        name="proj",
    )(h, pre, w_in, w_gate)


def _pool_kernel(xc_ref, xp_ref, pw_ref, scale_ref, o_ref):
    tiles_per_seq = SEQ // POOL_TM
    it = pl.program_id(0) % tiles_per_seq
    cur = xc_ref[...].astype(F32)
    prev = jnp.where(it == 0, 0.0, xp_ref[...].astype(F32))
    ext = jnp.concatenate([prev, cur], axis=0)
    t = it * POOL_TM + lax.broadcasted_iota(jnp.int32, (POOL_TM, 1), 0)
    for gi, w in enumerate(POOL_WINDOWS):
        cols = slice(gi * POOL_GROUP, (gi + 1) * POOL_GROUP)
        s = ext[:, cols]
        span = 1
        while span < w:
            s = s + pltpu.roll(s, span, 0)
            span *= 2
        wsum = s[POOL_HALO:, :]
        count = jnp.minimum(t + 1, w).astype(F32)
        pooled = wsum / count - cur[:, cols]
        mixed = jnp.dot(pooled.astype(BF16), pw_ref[gi], preferred_element_type=F32)
        o_ref[:, cols] = (mixed * scale_ref[:, cols]).astype(o_ref.dtype)


def _pool(p, pool_w, pool_scale, layer):
    halo_blocks = POOL_TM // POOL_HALO
    return pl.pallas_call(
        _pool_kernel,
        grid=(TOKENS // POOL_TM,),
        in_specs=[
            pl.BlockSpec((POOL_TM, POOL_WIDTH), lambda i: (i, 0)),
            pl.BlockSpec((POOL_HALO, POOL_WIDTH), lambda i: (jnp.maximum(i * halo_blocks - 1, 0), 0)),
            pl.BlockSpec((None, len(POOL_WINDOWS), POOL_GROUP, POOL_GROUP), lambda i: (layer, 0, 0, 0)),
            pl.BlockSpec((None, 1, POOL_WIDTH), lambda i: (layer, 0, 0)),
        ],
        out_specs=pl.BlockSpec((POOL_TM, POOL_WIDTH), lambda i: (i, 0)),
        out_shape=jax.ShapeDtypeStruct((TOKENS, POOL_WIDTH), BF16),
        compiler_params=_params("parallel"),
        name="pool",
    )(p, p, pool_w, pool_scale)


def _attn_kernel(sink_ref, q_ref, kc_ref, kp_ref, vc_ref, vp_ref, o_ref, bias_ref, s_ref, e_ref, denom_ref):
    ci = lax.broadcasted_iota(jnp.int32, (ATTN_BLOCK, ATTN_BLOCK), 0)
    qi = lax.broadcasted_iota(jnp.int32, (ATTN_BLOCK, ATTN_BLOCK), 1)
    from_prev = ci > qi

    @pl.when(pl.program_id(0) == 0)
    def _():
        dist = jnp.where(from_prev, qi + ATTN_BLOCK - ci, qi - ci).astype(F32)
        for h in range(ATTN_HEADS):
            alibi = -(2.0 ** (-8.0 * (h + 1) / ATTN_HEADS)) * dist
            bias_ref[0, h] = alibi
            bias_ref[1, h] = jnp.where(from_prev, -jnp.inf, alibi)

    seq_start = (pl.program_id(0) % (SEQ // ATTN_BLOCK) == 0).astype(jnp.int32)
    q = q_ref[...] * (ATTN_HEAD_DIM ** -0.5)
    kk = jnp.concatenate([kp_ref[...], kc_ref[...]], axis=0)
    vv = jnp.concatenate([vp_ref[...], vc_ref[...]], axis=0)
    vv_t = vv.astype(F32).T.astype(BF16)

    for h in range(ATTN_HEADS):
        kv_cols = slice((h // ATTN_GROUP) * ATTN_HEAD_DIM, (h // ATTN_GROUP + 1) * ATTN_HEAD_DIM)
        s2 = lax.dot_general(kk[:, kv_cols], q[:, h * ATTN_HEAD_DIM:(h + 1) * ATTN_HEAD_DIM],
                             (((1,), (1,)), ((), ())), preferred_element_type=F32)
        s_ref[h] = jnp.where(from_prev, s2[:ATTN_BLOCK], s2[ATTN_BLOCK:]) + bias_ref[seq_start, h]

    s = s_ref[...]
    sink = sink_ref[...]
    m = jnp.maximum(jnp.max(s, axis=1, keepdims=True), sink)
    e = jnp.exp(s - m)
    denom_ref[...] = jnp.sum(e, axis=1, keepdims=True) + jnp.exp(sink - m)
    eb = e.astype(BF16)
    zero = jnp.zeros_like(eb)
    e_ref[:, :ATTN_BLOCK, :] = jnp.where(from_prev, eb, zero)
    e_ref[:, ATTN_BLOCK:, :] = jnp.where(from_prev, zero, eb)

    for pair in range(ATTN_HEADS // 2):
        outs = []
        for h in (2 * pair, 2 * pair + 1):
            kv_rows = slice((h // ATTN_GROUP) * ATTN_HEAD_DIM, (h // ATTN_GROUP + 1) * ATTN_HEAD_DIM)
            pv_t = jnp.dot(vv_t[kv_rows], e_ref[h], preferred_element_type=F32)
            outs.append(pv_t / denom_ref[h])
        pair_cols = slice(2 * pair * ATTN_HEAD_DIM, (2 * pair + 2) * ATTN_HEAD_DIM)
        o_ref[:, pair_cols] = jnp.concatenate(outs, axis=0).T.astype(o_ref.dtype)


def _attn(p, sinks, layer):
    kcol = OFF_K // KV_WIDTH
    vcol = OFF_V // KV_WIDTH
    return pl.pallas_call(
        _attn_kernel,
        grid=(TOKENS // ATTN_BLOCK,),
        in_specs=[
            pl.BlockSpec((None, ATTN_HEADS, 1, 1), lambda r: (layer, 0, 0, 0)),
            pl.BlockSpec((ATTN_BLOCK, Q_WIDTH), lambda r: (r, OFF_Q // Q_WIDTH)),
            pl.BlockSpec((ATTN_BLOCK, KV_WIDTH), lambda r: (r, kcol)),
            pl.BlockSpec((ATTN_BLOCK, KV_WIDTH), lambda r: (jnp.maximum(r - 1, 0), kcol)),
            pl.BlockSpec((ATTN_BLOCK, KV_WIDTH), lambda r: (r, vcol)),
            pl.BlockSpec((ATTN_BLOCK, KV_WIDTH), lambda r: (jnp.maximum(r - 1, 0), vcol)),
        ],
        out_specs=pl.BlockSpec((ATTN_BLOCK, Q_WIDTH), lambda r: (r, 0)),
        out_shape=jax.ShapeDtypeStruct((TOKENS, Q_WIDTH), BF16),
        scratch_shapes=[
            pltpu.VMEM((2, ATTN_HEADS, ATTN_BLOCK, ATTN_BLOCK), F32),
            pltpu.VMEM((ATTN_HEADS, ATTN_BLOCK, ATTN_BLOCK), F32),
            pltpu.VMEM((ATTN_HEADS, 2 * ATTN_BLOCK, ATTN_BLOCK), BF16),
            pltpu.VMEM((ATTN_HEADS, 1, ATTN_BLOCK), F32),
        ],
        compiler_params=_params("arbitrary"),
        name="attn",
    )(sinks.reshape(DEPTH, ATTN_HEADS, 1, 1), p, p, p, p, p)


def _ret_kernel(logg_ref, *refs):
    nh = RET_HEADS
    q_refs, k_refs, v_refs, g_refs = (refs[i * nh:(i + 1) * nh] for i in range(4))
    norm_ref, o_ref, state_ref, intra_ref, qdec_ref, kdec_ref = refs[4 * nh:]
    key_scale = RET_KEY_DIM ** -0.5

    @pl.when((pl.program_id(0) == 0) & (pl.program_id(1) == 0))
    def _():
        pos = lax.broadcasted_iota(jnp.int32, (RET_CHUNK, 1), 0).astype(F32)
        row = lax.broadcasted_iota(jnp.int32, (RET_CHUNK, RET_CHUNK), 0)
        col = lax.broadcasted_iota(jnp.int32, (RET_CHUNK, RET_CHUNK), 1)
        diff = (row - col).astype(F32)
        for h in range(nh):
            lg = logg_ref[h]
            intra_ref[h] = jnp.where(diff >= 0, jnp.exp(lg * jnp.maximum(diff, 0.0)), 0.0) * key_scale
            qdec_ref[h] = jnp.exp(lg * (pos + 1.0))
            kdec_ref[h] = jnp.exp(lg * (RET_CHUNK - 1.0 - pos)) * key_scale

    @pl.when(pl.program_id(1) == 0)
    def _():
        state_ref[...] = jnp.zeros_like(state_ref)

    for h in range(nh):
        v_cols = slice(h * RET_VALUE_DIM, (h + 1) * RET_VALUE_DIM)
        chunk_decay = jnp.exp(jnp.full((1, 1), RET_CHUNK, F32) * logg_ref[h])
        q = q_refs[h][...]
        k = k_refs[h][...]
        v = v_refs[h][...]
        state = state_ref[h]
        inner = lax.dot_general(q, k, (((1,), (1,)), ((), ())), preferred_element_type=F32) * intra_ref[h]
        qd = (q.astype(F32) * qdec_ref[h]).astype(BF16)
        o = (jnp.dot(inner.astype(BF16), v, preferred_element_type=F32)
             + jnp.dot(qd, state.astype(BF16), preferred_element_type=F32))
        kd = (k.astype(F32) * kdec_ref[h]).astype(BF16)
        state_ref[h] = state * chunk_decay + lax.dot_general(
            kd, v, (((0,), (0,)), ((), ())), preferred_element_type=F32)

        o = o * _rms_scale(o) * norm_ref[:, v_cols]
        g = g_refs[h][...].astype(F32)
        o_ref[:, v_cols] = (g * jax.nn.sigmoid(g) * o).astype(o_ref.dtype)


def _retention(p, log_g, ret_norm, layer):
    nc = SEQ // RET_CHUNK

    def head_specs(width, offset):
        return [pl.BlockSpec((RET_CHUNK, width), lambda b, c, h=h: (b * nc + c, offset // width + h))
                for h in range(RET_HEADS)]

    return pl.pallas_call(
        _ret_kernel,
        grid=(BATCH, nc),
        in_specs=[pl.BlockSpec(memory_space=pltpu.SMEM)]
        + head_specs(RET_KEY_DIM, OFF_RQ) + head_specs(RET_KEY_DIM, OFF_RK)
        + head_specs(RET_VALUE_DIM, OFF_RV) + head_specs(RET_VALUE_DIM, OFF_RG)
        + [pl.BlockSpec((None, 1, RV_WIDTH), lambda b, c: (layer, 0, 0))],
        out_specs=pl.BlockSpec((RET_CHUNK, RV_WIDTH), lambda b, c: (b * nc + c, 0)),
        out_shape=jax.ShapeDtypeStruct((TOKENS, RV_WIDTH), BF16),
        scratch_shapes=[
            pltpu.VMEM((RET_HEADS, RET_KEY_DIM, RET_VALUE_DIM), F32),
            pltpu.VMEM((RET_HEADS, RET_CHUNK, RET_CHUNK), F32),
            pltpu.VMEM((RET_HEADS, RET_CHUNK, 1), F32),
            pltpu.VMEM((RET_HEADS, RET_CHUNK, 1), F32),
        ],
        compiler_params=_params("arbitrary", "arbitrary"),
        name="retention",
    )(log_g, *([p] * (4 * RET_HEADS)), ret_norm)


def _merge_kernel(h_ref, pm_ref, at_ref, rt_ref, gp_ref, ga_ref, gr_ref,
                  wp_ref, wa_ref, wr_ref, wo_ref, post_ref, o_ref, acc_ref, scale_ref):
    j = pl.program_id(1)

    @pl.when(j == 0)
    def _():
        acc_ref[...] = jnp.zeros_like(acc_ref)

    def gate(ref):
        return jax.nn.sigmoid(ref[...].astype(F32))

    merged = (gate(gp_ref) * jnp.dot(pm_ref[...], wp_ref[...], preferred_element_type=F32)
              + gate(ga_ref) * jnp.dot(at_ref[...], wa_ref[...], preferred_element_type=F32)
              + gate(gr_ref) * jnp.dot(rt_ref[...], wr_ref[...], preferred_element_type=F32))
    acc_ref[...] += jnp.dot(merged.astype(BF16), wo_ref[...], preferred_element_type=F32)

    @pl.when(j == pl.num_programs(1) - 1)
    def _():
        scale_ref[...] = _rms_scale(acc_ref[...])
        o_ref[...] = h_ref[...] + acc_ref[...] * scale_ref[...] * post_ref[...]


def _merge(h, pm, at, rt, p, w_pool_out, w_attn_out, w_ret_out, w_out, post, layer):
    nj = D_MODEL // MERGE_TJ
    gcol = OFF_GATE // MERGE_TJ
    return pl.pallas_call(
        _merge_kernel,
        grid=(TOKENS // MERGE_TM, nj),
        in_specs=[
            pl.BlockSpec((MERGE_TM, D_MODEL), lambda i, j: (i, 0)),
            pl.BlockSpec((MERGE_TM, POOL_WIDTH), lambda i, j: (i, 0)),
            pl.BlockSpec((MERGE_TM, Q_WIDTH), lambda i, j: (i, 0)),
            pl.BlockSpec((MERGE_TM, RV_WIDTH), lambda i, j: (i, 0)),
            pl.BlockSpec((MERGE_TM, MERGE_TJ), lambda i, j: (i, gcol + j)),
            pl.BlockSpec((MERGE_TM, MERGE_TJ), lambda i, j: (i, gcol + nj + j)),
            pl.BlockSpec((MERGE_TM, MERGE_TJ), lambda i, j: (i, gcol + 2 * nj + j)),
            pl.BlockSpec((None, POOL_WIDTH, MERGE_TJ), lambda i, j: (layer, 0, j)),
            pl.BlockSpec((None, Q_WIDTH, MERGE_TJ), lambda i, j: (layer, 0, j)),
            pl.BlockSpec((None, RV_WIDTH, MERGE_TJ), lambda i, j: (layer, 0, j)),
            pl.BlockSpec((None, MERGE_TJ, D_MODEL), lambda i, j: (layer, j, 0)),
            pl.BlockSpec((None, 1, D_MODEL), lambda i, j: (layer, 0, 0)),
        ],
        out_specs=pl.BlockSpec((MERGE_TM, D_MODEL), lambda i, j: (i, 0)),
        out_shape=jax.ShapeDtypeStruct((TOKENS, D_MODEL), F32),
        scratch_shapes=[pltpu.VMEM((MERGE_TM, D_MODEL), F32), pltpu.VMEM((MERGE_TM, 1), F32)],
        compiler_params=_params("parallel", "arbitrary"),
        name="merge",
    )(h, pm, at, rt, p, p, p, w_pool_out, w_attn_out, w_ret_out, w_out, post)


def kernel(x, ffn1_pre, ffn1_up, ffn1_down, ffn1_post, mix_pre, w_in, pool_w, pool_scale, attn_sinks, ret_norm, w_pool_out, w_attn_out, w_ret_out, w_gate, w_out, mix_post, ffn2_pre, ffn2_up, ffn2_down, ffn2_post):
    def gain(a):
        return a.reshape(DEPTH, 1, a.shape[-1])

    def bf(a):
        return a.astype(BF16)

    pool_w, w_pool_out, w_attn_out, w_ret_out, w_out = (
        bf(pool_w), bf(w_pool_out), bf(w_attn_out), bf(w_ret_out), bf(w_out))
    log_g = jnp.log(1.0 - jnp.exp2(-5.0 - jnp.arange(RET_HEADS, dtype=F32)))

    h = x.reshape(TOKENS, D_MODEL)
    for l in range(DEPTH):
        h = _ffn(h, gain(ffn1_pre), ffn1_up, ffn1_down, gain(ffn1_post), l)
        p = _proj(h, gain(mix_pre), w_in, w_gate, l)
        pm = _pool(p, pool_w, gain(pool_scale), l)
        at = _attn(p, attn_sinks, l)
        rt = _retention(p, log_g, gain(ret_norm), l)
        h = _merge(h, pm, at, rt, p, w_pool_out, w_attn_out, w_ret_out, w_out, gain(mix_post), l)
        h = _ffn(h, gain(ffn2_pre), ffn2_up, ffn2_down, gain(ffn2_post), l)
    return h.reshape(BATCH, SEQ, D_MODEL)
```

```python
import jax
import jax.numpy as jnp
from jax import lax
from jax.experimental import pallas as pl
from jax.experimental.pallas import tpu as pltpu

F32 = jnp.float32
BF16 = jnp.bfloat16

D_MODEL = 2048
BATCH = 4
SEQ = 2048
DEPTH = 2
TOKENS = BATCH * SEQ

POOL_WINDOWS = (2, 4, 8, 16)
POOL_GROUP = 256
POOL_WIDTH = 1024
POOL_HALO = 16
ATTN_HEADS = 16
ATTN_KV_HEADS = 4
ATTN_GROUP = ATTN_HEADS // ATTN_KV_HEADS
ATTN_HEAD_DIM = 64
ATTN_BLOCK = 128
Q_WIDTH = 1024
KV_WIDTH = 256
RET_HEADS = 4
RET_KEY_DIM = 256
RET_VALUE_DIM = 512
RQK_WIDTH = 1024
RV_WIDTH = 2048
FFN_HIDDEN = 5632
RMS_EPS = 1e-6
GATE_WIDTH = 3 * D_MODEL

OFF_XP = 0
OFF_Q = OFF_XP + POOL_WIDTH
OFF_K = OFF_Q + Q_WIDTH
OFF_V = OFF_K + KV_WIDTH
OFF_RQ = OFF_V + KV_WIDTH
OFF_RK = OFF_RQ + RQK_WIDTH
OFF_RV = OFF_RK + RQK_WIDTH
OFF_RG = OFF_RV + RV_WIDTH
IN_WIDTH = OFF_RG + RV_WIDTH

VMEM_LIMIT_BYTES = 60 * 1024 * 1024

FFN_TM = 1024
FFN_TF = 256
PROJ_TM = 2048
PROJ_TN = 512
MIX_TM = 1024
MIX_ROWS = ATTN_BLOCK
MIX_STEPS = MIX_TM // MIX_ROWS
GATE_TN = GATE_WIDTH // MIX_STEPS
MERGE_TM = 512
MERGE_TJ = 512


def _params(*sem):
    return pltpu.CompilerParams(dimension_semantics=sem, vmem_limit_bytes=VMEM_LIMIT_BYTES)


def _rms_scale(x):
    return lax.rsqrt(jnp.mean(x * x, axis=-1, keepdims=True) + RMS_EPS)


def _ffn_kernel(h_ref, pre_ref, wg_ref, wu_ref, wd_ref, post_ref, o_ref, xn_ref, scale_ref):
    j = pl.program_id(1)

    @pl.when(j == 0)
    def _():
        x = h_ref[...]
        xn_ref[...] = (x * _rms_scale(x) * pre_ref[...]).astype(BF16)
        o_ref[...] = jnp.zeros_like(o_ref)

    w_gu = jnp.concatenate([wg_ref[...].astype(BF16), wu_ref[...].astype(BF16)], axis=1)
    gu = jnp.dot(xn_ref[...], w_gu, preferred_element_type=F32)
    g = gu[:, :FFN_TF]
    a = (g * jax.nn.sigmoid(g) * gu[:, FFN_TF:]).astype(BF16)
    o_ref[...] += jnp.dot(a, wd_ref[...].astype(BF16), preferred_element_type=F32)

    @pl.when(j == pl.num_programs(1) - 1)
    def _():
        scale_ref[...] = _rms_scale(o_ref[...])
        o_ref[...] = h_ref[...] + 0.5 * (o_ref[...] * scale_ref[...] * post_ref[...])


def _ffn(h, pre, w_up, w_down, post, layer):
    nf = FFN_HIDDEN // FFN_TF
    return pl.pallas_call(
        _ffn_kernel,
        grid=(TOKENS // FFN_TM, nf),
        in_specs=[
            pl.BlockSpec((FFN_TM, D_MODEL), lambda i, j: (i, 0)),
            pl.BlockSpec((None, 1, D_MODEL), lambda i, j: (layer, 0, 0)),
            pl.BlockSpec((None, D_MODEL, FFN_TF), lambda i, j: (layer, 0, j)),
            pl.BlockSpec((None, D_MODEL, FFN_TF), lambda i, j: (layer, 0, nf + j)),
            pl.BlockSpec((None, FFN_TF, D_MODEL), lambda i, j: (layer, j, 0)),
            pl.BlockSpec((None, 1, D_MODEL), lambda i, j: (layer, 0, 0)),
        ],
        out_specs=pl.BlockSpec((FFN_TM, D_MODEL), lambda i, j: (i, 0)),
        out_shape=jax.ShapeDtypeStruct((TOKENS, D_MODEL), F32),
        scratch_shapes=[pltpu.VMEM((FFN_TM, D_MODEL), BF16), pltpu.VMEM((FFN_TM, 1), F32)],
        compiler_params=_params("parallel", "arbitrary"),
        name="ffn",
    )(h, pre, w_up, w_up, w_down, post)


def _proj_kernel(h_ref, pre_ref, w_ref, o_ref, xn_ref):
    @pl.when(pl.program_id(1) == 0)
    def _():
        x = h_ref[...]
        xn_ref[...] = (x * _rms_scale(x) * pre_ref[...]).astype(BF16)

    o_ref[...] = jnp.dot(xn_ref[...], w_ref[...].astype(BF16),
                         preferred_element_type=F32).astype(o_ref.dtype)


def _proj(h, pre, w_in, layer):
    return pl.pallas_call(
        _proj_kernel,
        grid=(TOKENS // PROJ_TM, IN_WIDTH // PROJ_TN),
        in_specs=[
            pl.BlockSpec((PROJ_TM, D_MODEL), lambda i, j: (i, 0), pipeline_mode=pl.Buffered(1)),
            pl.BlockSpec((None, 1, D_MODEL), lambda i, j: (layer, 0, 0)),
            pl.BlockSpec((None, D_MODEL, PROJ_TN), lambda i, j: (layer, 0, j)),
        ],
        out_specs=pl.BlockSpec((PROJ_TM, PROJ_TN), lambda i, j: (i, j)),
        out_shape=jax.ShapeDtypeStruct((TOKENS, IN_WIDTH), BF16),
        scratch_shapes=[pltpu.VMEM((PROJ_TM, D_MODEL), BF16)],
        compiler_params=_params("parallel", "arbitrary"),
        name="proj",
    )(h, pre, w_in)


def _pool_step(xc_ref, xp_ref, pw_ref, scale_ref, o_ref, seq_block):
    cur = xc_ref[...].astype(F32)
    prev = jnp.where(seq_block == 0, 0.0, xp_ref[...].astype(F32))
    ext = jnp.concatenate([prev, cur], axis=0)
    t = seq_block * MIX_ROWS + lax.broadcasted_iota(jnp.int32, (MIX_ROWS, 1), 0)
    for gi, w in enumerate(POOL_WINDOWS):
        cols = slice(gi * POOL_GROUP, (gi + 1) * POOL_GROUP)
        s = ext[:, cols]
        span = 1
        while span < w:
            s = s + pltpu.roll(s, span, 0)
            span *= 2
        wsum = s[POOL_HALO:, :]
        count = jnp.minimum(t + 1, w).astype(F32)
        pooled = wsum / count - cur[:, cols]
        mixed = jnp.dot(pooled.astype(BF16), pw_ref[gi], preferred_element_type=F32)
        o_ref[:, cols] = (mixed * scale_ref[:, cols]).astype(o_ref.dtype)


def _attn_fold_mask():
    ci = lax.broadcasted_iota(jnp.int32, (ATTN_BLOCK, ATTN_BLOCK), 0)
    qi = lax.broadcasted_iota(jnp.int32, (ATTN_BLOCK, ATTN_BLOCK), 1)
    return ci, qi, ci > qi


def _attn_init_bias(bias_ref):
    ci, qi, from_prev = _attn_fold_mask()
    dist = jnp.where(from_prev, qi + ATTN_BLOCK - ci, qi - ci).astype(F32)
    for h in range(ATTN_HEADS):
        alibi = -(2.0 ** (-8.0 * (h + 1) / ATTN_HEADS)) * dist
        bias_ref[0, h] = alibi
        bias_ref[1, h] = jnp.where(from_prev, -jnp.inf, alibi)


def _attn_step(sink_ref, q_ref, kc_ref, kp_ref, vc_ref, vp_ref, o_ref,
               bias_ref, s_ref, e_ref, denom_ref, seq_block):
    _, _, from_prev = _attn_fold_mask()
    seq_start = (seq_block == 0).astype(jnp.int32)
    q = q_ref[...] * (ATTN_HEAD_DIM ** -0.5)
    kk = jnp.concatenate([kp_ref[...], kc_ref[...]], axis=0)
    vv = jnp.concatenate([vp_ref[...], vc_ref[...]], axis=0)
    vv_t = vv.astype(F32).T.astype(BF16)

    for h in range(ATTN_HEADS):
        kv_cols = slice((h // ATTN_GROUP) * ATTN_HEAD_DIM, (h // ATTN_GROUP + 1) * ATTN_HEAD_DIM)
        s2 = lax.dot_general(kk[:, kv_cols], q[:, h * ATTN_HEAD_DIM:(h + 1) * ATTN_HEAD_DIM],
                             (((1,), (1,)), ((), ())), preferred_element_type=F32)
        s_ref[h] = jnp.where(from_prev, s2[:ATTN_BLOCK], s2[ATTN_BLOCK:]) + bias_ref[seq_start, h]

    s = s_ref[...]
    sink = sink_ref[...]
    m = jnp.maximum(jnp.max(s, axis=1, keepdims=True), sink)
    e = jnp.exp(s - m)
    denom_ref[...] = jnp.sum(e, axis=1, keepdims=True) + jnp.exp(sink - m)
    eb = e.astype(BF16)
    zero = jnp.zeros_like(eb)
    e_ref[:, :ATTN_BLOCK, :] = jnp.where(from_prev, eb, zero)
    e_ref[:, ATTN_BLOCK:, :] = jnp.where(from_prev, zero, eb)

    for pair in range(ATTN_HEADS // 2):
        outs = []
        for h in (2 * pair, 2 * pair + 1):
            kv_rows = slice((h // ATTN_GROUP) * ATTN_HEAD_DIM, (h // ATTN_GROUP + 1) * ATTN_HEAD_DIM)
            pv_t = jnp.dot(vv_t[kv_rows], e_ref[h], preferred_element_type=F32)
            outs.append(pv_t / denom_ref[h])
        pair_cols = slice(2 * pair * ATTN_HEAD_DIM, (2 * pair + 2) * ATTN_HEAD_DIM)
        o_ref[:, pair_cols] = jnp.concatenate(outs, axis=0).T.astype(o_ref.dtype)


def _ret_init_tables(logg_ref, intra_ref, qdec_ref, kdec_ref):
    key_scale = RET_KEY_DIM ** -0.5
    pos = lax.broadcasted_iota(jnp.int32, (MIX_ROWS, 1), 0).astype(F32)
    row = lax.broadcasted_iota(jnp.int32, (MIX_ROWS, MIX_ROWS), 0)
    col = lax.broadcasted_iota(jnp.int32, (MIX_ROWS, MIX_ROWS), 1)
    diff = (row - col).astype(F32)
    for h in range(RET_HEADS):
        lg = logg_ref[h]
        intra_ref[h] = jnp.where(diff >= 0, jnp.exp(lg * jnp.maximum(diff, 0.0)), 0.0) * key_scale
        qdec_ref[h] = jnp.exp(lg * (pos + 1.0))
        kdec_ref[h] = jnp.exp(lg * (MIX_ROWS - 1.0 - pos)) * key_scale


def _ret_step(logg_ref, q_refs, k_refs, v_refs, g_refs, norm_ref, o_ref,
              state_ref, intra_ref, qdec_ref, kdec_ref):
    for h in range(RET_HEADS):
        v_cols = slice(h * RET_VALUE_DIM, (h + 1) * RET_VALUE_DIM)
        chunk_decay = jnp.exp(jnp.full((1, 1), MIX_ROWS, F32) * logg_ref[h])
        q = q_refs[h][...]
        k = k_refs[h][...]
        v = v_refs[h][...]
        state = state_ref[h]
        inner = lax.dot_general(q, k, (((1,), (1,)), ((), ())), preferred_element_type=F32) * intra_ref[h]
        qd = (q.astype(F32) * qdec_ref[h]).astype(BF16)
        o = (jnp.dot(inner.astype(BF16), v, preferred_element_type=F32)
             + jnp.dot(qd, state.astype(BF16), preferred_element_type=F32))
        kd = (k.astype(F32) * kdec_ref[h]).astype(BF16)
        state_ref[h] = state * chunk_decay + lax.dot_general(
            kd, v, (((0,), (0,)), ((), ())), preferred_element_type=F32)

        o = o * _rms_scale(o) * norm_ref[:, v_cols]
        g = g_refs[h][...].astype(F32)
        o_ref[:, v_cols] = (g * jax.nn.sigmoid(g) * o).astype(o_ref.dtype)


def _mix_kernel(*refs):
    nh = RET_HEADS
    (logg_ref, h_ref, pre_ref, wgate_ref) = refs[:4]
    q_refs, k_refs, v_refs, g_refs = (refs[4 + i * nh:4 + (i + 1) * nh] for i in range(4))
    (norm_ref, sink_ref, aq_ref, kc_ref, kp_ref, vc_ref, vp_ref,
     xc_ref, xp_ref, pw_ref, pscale_ref,
     gates_ref, rt_ref, at_ref, pm_ref,
     xn_ref, state_ref, intra_ref, qdec_ref, kdec_ref,
     bias_ref, s_ref, e_ref, denom_ref) = refs[4 + 4 * nh:]

    i = pl.program_id(0)
    j = pl.program_id(1)
    seq_block = (i * MIX_STEPS + j) % (SEQ // MIX_ROWS)

    @pl.when((i == 0) & (j == 0))
    def _():
        _ret_init_tables(logg_ref, intra_ref, qdec_ref, kdec_ref)
        _attn_init_bias(bias_ref)

    @pl.when(seq_block == 0)
    def _():
        state_ref[...] = jnp.zeros_like(state_ref)

    @pl.when(j == 0)
    def _():
        x = h_ref[...]
        xn_ref[...] = (x * _rms_scale(x) * pre_ref[...]).astype(BF16)

    gates_ref[...] = jnp.dot(xn_ref[...], wgate_ref[...].astype(BF16),
                             preferred_element_type=F32).astype(gates_ref.dtype)
    _ret_step(logg_ref, q_refs, k_refs, v_refs, g_refs, norm_ref, rt_ref,
              state_ref, intra_ref, qdec_ref, kdec_ref)
    _attn_step(sink_ref, aq_ref, kc_ref, kp_ref, vc_ref, vp_ref, at_ref,
               bias_ref, s_ref, e_ref, denom_ref, seq_block)
    _pool_step(xc_ref, xp_ref, pw_ref, pscale_ref, pm_ref, seq_block)


def _mix(h, pre, w_gate, p, log_g, ret_norm, sinks, pool_w, pool_scale, layer):
    def row(i, j):
        return i * MIX_STEPS + j

    def prev_row(i, j):
        return jnp.maximum(row(i, j) - 1, 0)

    def head_specs(width, offset):
        return [pl.BlockSpec((MIX_ROWS, width), lambda i, j, h=h: (row(i, j), offset // width + h))
                for h in range(RET_HEADS)]

    halo_blocks = MIX_ROWS // POOL_HALO
    kcol = OFF_K // KV_WIDTH
    vcol = OFF_V // KV_WIDTH
    in_specs = (
        [pl.BlockSpec(memory_space=pltpu.SMEM),
         pl.BlockSpec((MIX_TM, D_MODEL), lambda i, j: (i, 0)),
         pl.BlockSpec((None, 1, D_MODEL), lambda i, j: (layer, 0, 0)),
         pl.BlockSpec((None, D_MODEL, GATE_TN), lambda i, j: (layer, 0, j))]
        + head_specs(RET_KEY_DIM, OFF_RQ) + head_specs(RET_KEY_DIM, OFF_RK)
        + head_specs(RET_VALUE_DIM, OFF_RV) + head_specs(RET_VALUE_DIM, OFF_RG)
        + [pl.BlockSpec((None, 1, RV_WIDTH), lambda i, j: (layer, 0, 0)),
           pl.BlockSpec((None, ATTN_HEADS, 1, 1), lambda i, j: (layer, 0, 0, 0)),
           pl.BlockSpec((MIX_ROWS, Q_WIDTH), lambda i, j: (row(i, j), OFF_Q // Q_WIDTH)),
           pl.BlockSpec((MIX_ROWS, KV_WIDTH), lambda i, j: (row(i, j), kcol)),
           pl.BlockSpec((MIX_ROWS, KV_WIDTH), lambda i, j: (prev_row(i, j), kcol)),
           pl.BlockSpec((MIX_ROWS, KV_WIDTH), lambda i, j: (row(i, j), vcol)),
           pl.BlockSpec((MIX_ROWS, KV_WIDTH), lambda i, j: (prev_row(i, j), vcol)),
           pl.BlockSpec((MIX_ROWS, POOL_WIDTH), lambda i, j: (row(i, j), 0)),
           pl.BlockSpec((POOL_HALO, POOL_WIDTH),
                        lambda i, j: (jnp.maximum(row(i, j) * halo_blocks - 1, 0), 0)),
           pl.BlockSpec((None, len(POOL_WINDOWS), POOL_GROUP, POOL_GROUP), lambda i, j: (layer, 0, 0, 0)),
           pl.BlockSpec((None, 1, POOL_WIDTH), lambda i, j: (layer, 0, 0))])
    n_p_operands = 4 * RET_HEADS
    return pl.pallas_call(
        _mix_kernel,
        grid=(TOKENS // MIX_TM, MIX_STEPS),
        in_specs=in_specs,
        out_specs=[
            pl.BlockSpec((MIX_TM, GATE_TN), lambda i, j: (i, j)),
            pl.BlockSpec((MIX_ROWS, RV_WIDTH), lambda i, j: (row(i, j), 0)),
            pl.BlockSpec((MIX_ROWS, Q_WIDTH), lambda i, j: (row(i, j), 0)),
            pl.BlockSpec((MIX_ROWS, POOL_WIDTH), lambda i, j: (row(i, j), 0)),
        ],
        out_shape=[
            jax.ShapeDtypeStruct((TOKENS, GATE_WIDTH), BF16),
            jax.ShapeDtypeStruct((TOKENS, RV_WIDTH), BF16),
            jax.ShapeDtypeStruct((TOKENS, Q_WIDTH), BF16),
            jax.ShapeDtypeStruct((TOKENS, POOL_WIDTH), BF16),
        ],
        scratch_shapes=[
            pltpu.VMEM((MIX_TM, D_MODEL), BF16),
            pltpu.VMEM((RET_HEADS, RET_KEY_DIM, RET_VALUE_DIM), F32),
            pltpu.VMEM((RET_HEADS, MIX_ROWS, MIX_ROWS), F32),
            pltpu.VMEM((RET_HEADS, MIX_ROWS, 1), F32),
            pltpu.VMEM((RET_HEADS, MIX_ROWS, 1), F32),
            pltpu.VMEM((2, ATTN_HEADS, ATTN_BLOCK, ATTN_BLOCK), F32),
            pltpu.VMEM((ATTN_HEADS, ATTN_BLOCK, ATTN_BLOCK), F32),
            pltpu.VMEM((ATTN_HEADS, 2 * ATTN_BLOCK, ATTN_BLOCK), BF16),
            pltpu.VMEM((ATTN_HEADS, 1, ATTN_BLOCK), F32),
        ],
        compiler_params=_params("arbitrary", "arbitrary"),
        name="mix",
    )(log_g, h, pre, w_gate, *([p] * n_p_operands), ret_norm, sinks,
      p, p, p, p, p, p, p, pool_w, pool_scale)


def _merge_kernel(h_ref, pm_ref, at_ref, rt_ref, gp_ref, ga_ref, gr_ref,
                  wp_ref, wa_ref, wr_ref, wo_ref, post_ref, o_ref, acc_ref, scale_ref):
    j = pl.program_id(1)

    @pl.when(j == 0)
    def _():
        acc_ref[...] = jnp.zeros_like(acc_ref)

    def gate(ref):
        return jax.nn.sigmoid(ref[...].astype(F32))

    merged = (gate(gp_ref) * jnp.dot(pm_ref[...], wp_ref[...], preferred_element_type=F32)
              + gate(ga_ref) * jnp.dot(at_ref[...], wa_ref[...], preferred_element_type=F32)
              + gate(gr_ref) * jnp.dot(rt_ref[...], wr_ref[...], preferred_element_type=F32))
    acc_ref[...] += jnp.dot(merged.astype(BF16), wo_ref[...], preferred_element_type=F32)

    @pl.when(j == pl.num_programs(1) - 1)
    def _():
        scale_ref[...] = _rms_scale(acc_ref[...])
        o_ref[...] = h_ref[...] + acc_ref[...] * scale_ref[...] * post_ref[...]


def _merge(h, pm, at, rt, gates, w_pool_out, w_attn_out, w_ret_out, w_out, post, layer):
    nj = D_MODEL // MERGE_TJ
    return pl.pallas_call(
        _merge_kernel,
        grid=(TOKENS // MERGE_TM, nj),
        in_specs=[
            pl.BlockSpec((MERGE_TM, D_MODEL), lambda i, j: (i, 0)),
            pl.BlockSpec((MERGE_TM, POOL_WIDTH), lambda i, j: (i, 0)),
            pl.BlockSpec((MERGE_TM, Q_WIDTH), lambda i, j: (i, 0)),
            pl.BlockSpec((MERGE_TM, RV_WIDTH), lambda i, j: (i, 0)),
            pl.BlockSpec((MERGE_TM, MERGE_TJ), lambda i, j: (i, j)),
            pl.BlockSpec((MERGE_TM, MERGE_TJ), lambda i, j: (i, nj + j)),
            pl.BlockSpec((MERGE_TM, MERGE_TJ), lambda i, j: (i, 2 * nj + j)),
            pl.BlockSpec((None, POOL_WIDTH, MERGE_TJ), lambda i, j: (layer, 0, j)),
            pl.BlockSpec((None, Q_WIDTH, MERGE_TJ), lambda i, j: (layer, 0, j)),
            pl.BlockSpec((None, RV_WIDTH, MERGE_TJ), lambda i, j: (layer, 0, j)),
            pl.BlockSpec((None, MERGE_TJ, D_MODEL), lambda i, j: (layer, j, 0)),
            pl.BlockSpec((None, 1, D_MODEL), lambda i, j: (layer, 0, 0)),
        ],
        out_specs=pl.BlockSpec((MERGE_TM, D_MODEL), lambda i, j: (i, 0)),
        out_shape=jax.ShapeDtypeStruct((TOKENS, D_MODEL), F32),
        scratch_shapes=[pltpu.VMEM((MERGE_TM, D_MODEL), F32), pltpu.VMEM((MERGE_TM, 1), F32)],
        compiler_params=_params("parallel", "arbitrary"),
        name="merge",
    )(h, pm, at, rt, gates, gates, gates, w_pool_out, w_attn_out, w_ret_out, w_out, post)


def kernel(x, ffn1_pre, ffn1_up, ffn1_down, ffn1_post, mix_pre, w_in, pool_w, pool_scale, attn_sinks, ret_norm, w_pool_out, w_attn_out, w_ret_out, w_gate, w_out, mix_post, ffn2_pre, ffn2_up, ffn2_down, ffn2_post):
    def gain(a):
        return a.reshape(DEPTH, 1, a.shape[-1])

    def bf(a):
        return a.astype(BF16)

    pool_w, w_pool_out, w_attn_out, w_ret_out, w_out = (
        bf(pool_w), bf(w_pool_out), bf(w_attn_out), bf(w_ret_out), bf(w_out))
    log_g = jnp.log(1.0 - jnp.exp2(-5.0 - jnp.arange(RET_HEADS, dtype=F32)))
    sinks = attn_sinks.reshape(DEPTH, ATTN_HEADS, 1, 1)

    h = x.reshape(TOKENS, D_MODEL)
    for l in range(DEPTH):
        h = _ffn(h, gain(ffn1_pre), ffn1_up, ffn1_down, gain(ffn1_post), l)
        p = _proj(h, gain(mix_pre), w_in, l)
        gates, rt, at, pm = _mix(h, gain(mix_pre), w_gate, p, log_g, gain(ret_norm), sinks,
                                 pool_w, gain(pool_scale), l)
        h = _merge(h, pm, at, rt, gates, w_pool_out, w_attn_out, w_ret_out, w_out, gain(mix_post), l)
        h = _ffn(h, gain(ffn2_pre), ffn2_up, ffn2_down, gain(ffn2_post), l)
    return h.reshape(BATCH, SEQ, D_MODEL)
```

```python
import jax
import jax.numpy as jnp
from jax import lax
from jax.experimental import pallas as pl
from jax.experimental.pallas import tpu as pltpu

F32 = jnp.float32
BF16 = jnp.bfloat16

D_MODEL = 2048
BATCH = 4
SEQ = 2048
DEPTH = 2
TOKENS = BATCH * SEQ

POOL_WINDOWS = (2, 4, 8, 16)
POOL_GROUP = 256
POOL_WIDTH = 1024
POOL_HALO = 16
ATTN_HEADS = 16
ATTN_KV_HEADS = 4
ATTN_GROUP = ATTN_HEADS // ATTN_KV_HEADS
ATTN_HEAD_DIM = 64
ATTN_BLOCK = 128
Q_WIDTH = 1024
KV_WIDTH = 256
RET_HEADS = 4
RET_KEY_DIM = 256
RET_VALUE_DIM = 512
RQK_WIDTH = 1024
RV_WIDTH = 2048
FFN_HIDDEN = 5632
RMS_EPS = 1e-6
GATE_WIDTH = 3 * D_MODEL

OFF_XP = 0
OFF_Q = OFF_XP + POOL_WIDTH
OFF_RQ = OFF_Q + Q_WIDTH
OFF_RK = OFF_RQ + RQK_WIDTH
OFF_RV = OFF_RK + RQK_WIDTH
OFF_RG = OFF_RV + RV_WIDTH
OFF_K = OFF_RG + RV_WIDTH
OFF_V = OFF_K + KV_WIDTH
IN_WIDTH = OFF_V + KV_WIDTH
W_IN_KV_OFFSET = POOL_WIDTH + Q_WIDTH

VMEM_LIMIT_BYTES = 60 * 1024 * 1024

FFN_TM = 1024
FFN_TF = 256
PROJ_TM = 2048
PROJ_TN = 512
MIX_TM = 1024
MIX_ROWS = ATTN_BLOCK
MIX_STEPS = MIX_TM // MIX_ROWS
GATE_TN = GATE_WIDTH // MIX_STEPS
MERGE_TM = 256
MERGE_TJ = 512


def _params(*sem):
    return pltpu.CompilerParams(dimension_semantics=sem, vmem_limit_bytes=VMEM_LIMIT_BYTES)


def _rms_scale(x):
    return lax.rsqrt(jnp.mean(x * x, axis=-1, keepdims=True) + RMS_EPS)


def _ffn_kernel(h_ref, pre_ref, wg_ref, wu_ref, wd_ref, post_ref, o_ref, xn_ref, scale_ref):
    j = pl.program_id(1)

    @pl.when(j == 0)
    def _():
        x = h_ref[...]
        xn_ref[...] = (x * _rms_scale(x) * pre_ref[...]).astype(BF16)
        o_ref[...] = jnp.zeros_like(o_ref)

    w_gu = jnp.concatenate([wg_ref[...].astype(BF16), wu_ref[...].astype(BF16)], axis=1)
    gu = jnp.dot(xn_ref[...], w_gu, preferred_element_type=F32)
    g = gu[:, :FFN_TF]
    a = (g * jax.nn.sigmoid(g) * gu[:, FFN_TF:]).astype(BF16)
    o_ref[...] += jnp.dot(a, wd_ref[...].astype(BF16), preferred_element_type=F32)

    @pl.when(j == pl.num_programs(1) - 1)
    def _():
        scale_ref[...] = _rms_scale(o_ref[...])
        o_ref[...] = h_ref[...] + 0.5 * (o_ref[...] * scale_ref[...] * post_ref[...])


def _ffn(h, pre, w_up, w_down, post, layer):
    nf = FFN_HIDDEN // FFN_TF
    return pl.pallas_call(
        _ffn_kernel,
        grid=(TOKENS // FFN_TM, nf),
        in_specs=[
            pl.BlockSpec((FFN_TM, D_MODEL), lambda i, j: (i, 0)),
            pl.BlockSpec((None, 1, D_MODEL), lambda i, j: (layer, 0, 0)),
            pl.BlockSpec((None, D_MODEL, FFN_TF), lambda i, j: (layer, 0, j)),
            pl.BlockSpec((None, D_MODEL, FFN_TF), lambda i, j: (layer, 0, nf + j)),
            pl.BlockSpec((None, FFN_TF, D_MODEL), lambda i, j: (layer, j, 0)),
            pl.BlockSpec((None, 1, D_MODEL), lambda i, j: (layer, 0, 0)),
        ],
        out_specs=pl.BlockSpec((FFN_TM, D_MODEL), lambda i, j: (i, 0)),
        out_shape=jax.ShapeDtypeStruct((TOKENS, D_MODEL), F32),
        scratch_shapes=[pltpu.VMEM((FFN_TM, D_MODEL), BF16), pltpu.VMEM((FFN_TM, 1), F32)],
        compiler_params=_params("parallel", "arbitrary"),
        name="ffn",
    )(h, pre, w_up, w_up, w_down, post)


def _proj_kernel(h_ref, pre_ref, w_ref, o_ref, xn_ref):
    @pl.when(pl.program_id(1) == 0)
    def _():
        x = h_ref[...]
        xn_ref[...] = (x * _rms_scale(x) * pre_ref[...]).astype(BF16)

    o_ref[...] = jnp.dot(xn_ref[...], w_ref[...].astype(BF16),
                         preferred_element_type=F32).astype(o_ref.dtype)


def _proj(h, pre, w_in, layer):
    kv_tile = W_IN_KV_OFFSET // PROJ_TN
    last_tile = IN_WIDTH // PROJ_TN - 1

    def w_tile(j):
        return jnp.where(j < kv_tile, j, jnp.where(j < last_tile, j + 1, kv_tile))

    return pl.pallas_call(
        _proj_kernel,
        grid=(TOKENS // PROJ_TM, IN_WIDTH // PROJ_TN),
        in_specs=[
            pl.BlockSpec((PROJ_TM, D_MODEL), lambda i, j: (i, 0)),
            pl.BlockSpec((None, 1, D_MODEL), lambda i, j: (layer, 0, 0)),
            pl.BlockSpec((None, D_MODEL, PROJ_TN), lambda i, j: (layer, 0, w_tile(j))),
        ],
        out_specs=pl.BlockSpec((PROJ_TM, PROJ_TN), lambda i, j: (i, j)),
        out_shape=jax.ShapeDtypeStruct((TOKENS, IN_WIDTH), BF16),
        scratch_shapes=[pltpu.VMEM((PROJ_TM, D_MODEL), BF16)],
        compiler_params=_params("parallel", "arbitrary"),
        name="proj",
    )(h, pre, w_in)


def _pool_step(xc_ref, xp_ref, pw_ref, scale_ref, o_ref, seq_block):
    cur = xc_ref[...].astype(F32)
    prev = jnp.where(seq_block == 0, 0.0, xp_ref[...].astype(F32))
    ext = jnp.concatenate([prev, cur], axis=0)
    t = seq_block * MIX_ROWS + lax.broadcasted_iota(jnp.int32, (MIX_ROWS, 1), 0)
    for gi, w in enumerate(POOL_WINDOWS):
        cols = slice(gi * POOL_GROUP, (gi + 1) * POOL_GROUP)
        s = ext[:, cols]
        span = 1
        while span < w:
            s = s + pltpu.roll(s, span, 0)
            span *= 2
        wsum = s[POOL_HALO:, :]
        count = jnp.minimum(t + 1, w).astype(F32)
        pooled = wsum / count - cur[:, cols]
        mixed = jnp.dot(pooled.astype(BF16), pw_ref[gi], preferred_element_type=F32)
        o_ref[:, cols] = (mixed * scale_ref[:, cols]).astype(o_ref.dtype)


def _attn_fold_mask():
    ci = lax.broadcasted_iota(jnp.int32, (ATTN_BLOCK, ATTN_BLOCK), 0)
    qi = lax.broadcasted_iota(jnp.int32, (ATTN_BLOCK, ATTN_BLOCK), 1)
    return ci, qi, ci > qi


def _attn_init_bias(bias_ref):
    ci, qi, from_prev = _attn_fold_mask()
    dist = jnp.where(from_prev, qi + ATTN_BLOCK - ci, qi - ci).astype(F32)
    for h in range(ATTN_HEADS):
        alibi = -(2.0 ** (-8.0 * (h + 1) / ATTN_HEADS)) * dist
        bias_ref[0, h] = alibi
        bias_ref[1, h] = jnp.where(from_prev, -jnp.inf, alibi)


def _attn_step(sink_ref, q_ref, kc_ref, kp_ref, vc_ref, vp_ref, o_ref,
               bias_ref, s_ref, e_ref, denom_ref, seq_block):
    _, _, from_prev = _attn_fold_mask()
    seq_start = (seq_block == 0).astype(jnp.int32)
    q = q_ref[...] * (ATTN_HEAD_DIM ** -0.5)
    kk = jnp.concatenate([kp_ref[...], kc_ref[...]], axis=0)
    vv = jnp.concatenate([vp_ref[...], vc_ref[...]], axis=0)
    vv_t = vv.astype(F32).T.astype(BF16)

    for h in range(ATTN_HEADS):
        kv_cols = slice((h // ATTN_GROUP) * ATTN_HEAD_DIM, (h // ATTN_GROUP + 1) * ATTN_HEAD_DIM)
        s2 = lax.dot_general(kk[:, kv_cols], q[:, h * ATTN_HEAD_DIM:(h + 1) * ATTN_HEAD_DIM],
                             (((1,), (1,)), ((), ())), preferred_element_type=F32)
        s_ref[h] = jnp.where(from_prev, s2[:ATTN_BLOCK], s2[ATTN_BLOCK:]) + bias_ref[seq_start, h]

    s = s_ref[...]
    sink = sink_ref[...]
    m = jnp.maximum(jnp.max(s, axis=1, keepdims=True), sink)
    e = jnp.exp(s - m)
    denom_ref[...] = jnp.sum(e, axis=1, keepdims=True) + jnp.exp(sink - m)
    eb = e.astype(BF16)
    zero = jnp.zeros_like(eb)
    e_ref[:, :ATTN_BLOCK, :] = jnp.where(from_prev, eb, zero)
    e_ref[:, ATTN_BLOCK:, :] = jnp.where(from_prev, zero, eb)

    for pair in range(ATTN_HEADS // 2):
        outs = []
        for h in (2 * pair, 2 * pair + 1):
            kv_rows = slice((h // ATTN_GROUP) * ATTN_HEAD_DIM, (h // ATTN_GROUP + 1) * ATTN_HEAD_DIM)
            pv_t = jnp.dot(vv_t[kv_rows], e_ref[h], preferred_element_type=F32)
            outs.append(pv_t / denom_ref[h])
        pair_cols = slice(2 * pair * ATTN_HEAD_DIM, (2 * pair + 2) * ATTN_HEAD_DIM)
        o_ref[:, pair_cols] = jnp.concatenate(outs, axis=0).T.astype(o_ref.dtype)


def _ret_init_tables(logg_ref, intra_ref, qdec_ref, kdec_ref):
    key_scale = RET_KEY_DIM ** -0.5
    pos = lax.broadcasted_iota(jnp.int32, (MIX_ROWS, 1), 0).astype(F32)
    row = lax.broadcasted_iota(jnp.int32, (MIX_ROWS, MIX_ROWS), 0)
    col = lax.broadcasted_iota(jnp.int32, (MIX_ROWS, MIX_ROWS), 1)
    diff = (row - col).astype(F32)
    for h in range(RET_HEADS):
        lg = logg_ref[h]
        intra_ref[h] = jnp.where(diff >= 0, jnp.exp(lg * jnp.maximum(diff, 0.0)), 0.0) * key_scale
        qdec_ref[h] = jnp.exp(lg * (pos + 1.0))
        kdec_ref[h] = jnp.exp(lg * (MIX_ROWS - 1.0 - pos)) * key_scale


def _ret_step(logg_ref, q_ref, k_ref, v_ref, g_ref, norm_ref, o_ref,
              state_ref, intra_ref, qdec_ref, kdec_ref):
    for h in range(RET_HEADS):
        k_cols = slice(h * RET_KEY_DIM, (h + 1) * RET_KEY_DIM)
        v_cols = slice(h * RET_VALUE_DIM, (h + 1) * RET_VALUE_DIM)
        chunk_decay = jnp.exp(jnp.full((1, 1), MIX_ROWS, F32) * logg_ref[h])
        q = q_ref[:, k_cols]
        k = k_ref[:, k_cols]
        v = v_ref[:, v_cols]
        state = state_ref[h]
        inner = lax.dot_general(q, k, (((1,), (1,)), ((), ())), preferred_element_type=F32) * intra_ref[h]
        qd = (q.astype(F32) * qdec_ref[h]).astype(BF16)
        o = (jnp.dot(inner.astype(BF16), v, preferred_element_type=F32)
             + jnp.dot(qd, state.astype(BF16), preferred_element_type=F32))
        kd = (k.astype(F32) * kdec_ref[h]).astype(BF16)
        state_ref[h] = state * chunk_decay + lax.dot_general(
            kd, v, (((0,), (0,)), ((), ())), preferred_element_type=F32)

        o = o * _rms_scale(o) * norm_ref[:, v_cols]
        g = g_ref[:, v_cols].astype(F32)
        o_ref[:, v_cols] = (g * jax.nn.sigmoid(g) * o).astype(o_ref.dtype)


def _mix_kernel(logg_ref, h_ref, pre_ref, wgate_ref,
                rq_ref, rk_ref, rv_ref, rg_ref, norm_ref,
                sink_ref, aq_ref, kc_ref, kp_ref, vc_ref, vp_ref,
                xc_ref, xp_ref, pw_ref, pscale_ref,
                gates_ref, rt_ref, at_ref, pm_ref,
                xn_ref, state_ref, intra_ref, qdec_ref, kdec_ref,
                bias_ref, s_ref, e_ref, denom_ref):
    i = pl.program_id(0)
    j = pl.program_id(1)
    seq_block = (i * MIX_STEPS + j) % (SEQ // MIX_ROWS)

    @pl.when((i == 0) & (j == 0))
    def _():
        _ret_init_tables(logg_ref, intra_ref, qdec_ref, kdec_ref)
        _attn_init_bias(bias_ref)

    @pl.when(seq_block == 0)
    def _():
        state_ref[...] = jnp.zeros_like(state_ref)

    @pl.when(j == 0)
    def _():
        x = h_ref[...]
        xn_ref[...] = (x * _rms_scale(x) * pre_ref[...]).astype(BF16)

    gates_ref[...] = jnp.dot(xn_ref[...], wgate_ref[...].astype(BF16),
                             preferred_element_type=F32).astype(gates_ref.dtype)
    _ret_step(logg_ref, rq_ref, rk_ref, rv_ref, rg_ref, norm_ref, rt_ref,
              state_ref, intra_ref, qdec_ref, kdec_ref)
    _attn_step(sink_ref, aq_ref, kc_ref, kp_ref, vc_ref, vp_ref, at_ref,
               bias_ref, s_ref, e_ref, denom_ref, seq_block)
    _pool_step(xc_ref, xp_ref, pw_ref, pscale_ref, pm_ref, seq_block)


def _mix(h, pre, w_gate, p, log_g, ret_norm, sinks, pool_w, pool_scale, layer):
    def row(i, j):
        return i * MIX_STEPS + j

    def prev_row(i, j):
        return jnp.maximum(row(i, j) - 1, 0)

    def slab(width, offset):
        return pl.BlockSpec((MIX_ROWS, width), lambda i, j: (row(i, j), offset // width))

    halo_blocks = MIX_ROWS // POOL_HALO
    kcol = OFF_K // KV_WIDTH
    vcol = OFF_V // KV_WIDTH
    in_specs = [
        pl.BlockSpec(memory_space=pltpu.SMEM),
        pl.BlockSpec((MIX_TM, D_MODEL), lambda i, j: (i, 0)),
        pl.BlockSpec((None, 1, D_MODEL), lambda i, j: (layer, 0, 0)),
        pl.BlockSpec((None, D_MODEL, GATE_TN), lambda i, j: (layer, 0, j)),
        slab(RQK_WIDTH, OFF_RQ), slab(RQK_WIDTH, OFF_RK), slab(RV_WIDTH, OFF_RV), slab(RV_WIDTH, OFF_RG),
        pl.BlockSpec((None, 1, RV_WIDTH), lambda i, j: (layer, 0, 0)),
        pl.BlockSpec((None, ATTN_HEADS, 1, 1), lambda i, j: (layer, 0, 0, 0)),
        slab(Q_WIDTH, OFF_Q),
        pl.BlockSpec((MIX_ROWS, KV_WIDTH), lambda i, j: (row(i, j), kcol)),
        pl.BlockSpec((MIX_ROWS, KV_WIDTH), lambda i, j: (prev_row(i, j), kcol)),
        pl.BlockSpec((MIX_ROWS, KV_WIDTH), lambda i, j: (row(i, j), vcol)),
        pl.BlockSpec((MIX_ROWS, KV_WIDTH), lambda i, j: (prev_row(i, j), vcol)),
        slab(POOL_WIDTH, OFF_XP),
        pl.BlockSpec((POOL_HALO, POOL_WIDTH),
                     lambda i, j: (jnp.maximum(row(i, j) * halo_blocks - 1, 0), 0)),
        pl.BlockSpec((None, len(POOL_WINDOWS), POOL_GROUP, POOL_GROUP), lambda i, j: (layer, 0, 0, 0)),
        pl.BlockSpec((None, 1, POOL_WIDTH), lambda i, j: (layer, 0, 0)),
    ]
    return pl.pallas_call(
        _mix_kernel,
        grid=(TOKENS // MIX_TM, MIX_STEPS),
        in_specs=in_specs,
        out_specs=[
            pl.BlockSpec((MIX_TM, GATE_TN), lambda i, j: (i, j)),
            pl.BlockSpec((MIX_ROWS, RV_WIDTH), lambda i, j: (row(i, j), 0)),
            pl.BlockSpec((MIX_ROWS, Q_WIDTH), lambda i, j: (row(i, j), 0)),
            pl.BlockSpec((MIX_ROWS, POOL_WIDTH), lambda i, j: (row(i, j), 0)),
        ],
        out_shape=[
            jax.ShapeDtypeStruct((TOKENS, GATE_WIDTH), BF16),
            jax.ShapeDtypeStruct((TOKENS, RV_WIDTH), BF16),
            jax.ShapeDtypeStruct((TOKENS, Q_WIDTH), BF16),
            jax.ShapeDtypeStruct((TOKENS, POOL_WIDTH), BF16),
        ],
        scratch_shapes=[
            pltpu.VMEM((MIX_TM, D_MODEL), BF16),
            pltpu.VMEM((RET_HEADS, RET_KEY_DIM, RET_VALUE_DIM), F32),
            pltpu.VMEM((RET_HEADS, MIX_ROWS, MIX_ROWS), F32),
            pltpu.VMEM((RET_HEADS, MIX_ROWS, 1), F32),
            pltpu.VMEM((RET_HEADS, MIX_ROWS, 1), F32),
            pltpu.VMEM((2, ATTN_HEADS, ATTN_BLOCK, ATTN_BLOCK), F32),
            pltpu.VMEM((ATTN_HEADS, ATTN_BLOCK, ATTN_BLOCK), F32),
            pltpu.VMEM((ATTN_HEADS, 2 * ATTN_BLOCK, ATTN_BLOCK), BF16),
            pltpu.VMEM((ATTN_HEADS, 1, ATTN_BLOCK), F32),
        ],
        compiler_params=_params("arbitrary", "arbitrary"),
        name="mix",
    )(log_g, h, pre, w_gate, p, p, p, p, ret_norm, sinks,
      p, p, p, p, p, p, p, pool_w, pool_scale)


def _merge_kernel(h_ref, pm_ref, at_ref, rt_ref, gates_ref,
                  wp_ref, wa_ref, wr_ref, wo_ref, post_ref, o_ref, acc_ref, scale_ref):
    def gate(branch, cols):
        return jax.nn.sigmoid(gates_ref[:, branch * D_MODEL + cols.start:branch * D_MODEL + cols.stop]
                              .astype(F32))

    for jt in range(D_MODEL // MERGE_TJ):
        cols = slice(jt * MERGE_TJ, (jt + 1) * MERGE_TJ)
        merged = (gate(0, cols) * jnp.dot(pm_ref[...], wp_ref[:, cols], preferred_element_type=F32)
                  + gate(1, cols) * jnp.dot(at_ref[...], wa_ref[:, cols], preferred_element_type=F32)
                  + gate(2, cols) * jnp.dot(rt_ref[...], wr_ref[:, cols], preferred_element_type=F32))
        part = jnp.dot(merged.astype(BF16), wo_ref[cols, :], preferred_element_type=F32)
        if jt == 0:
            acc_ref[...] = part
        else:
            acc_ref[...] += part

    scale_ref[...] = _rms_scale(acc_ref[...])
    o_ref[...] = h_ref[...] + acc_ref[...] * scale_ref[...] * post_ref[...]


def _merge(h, pm, at, rt, gates, w_pool_out, w_attn_out, w_ret_out, w_out, post, layer):
    def resident(rows, cols):
        return pl.BlockSpec((None, rows, cols), lambda i: (layer, 0, 0), pipeline_mode=pl.Buffered(1))

    return pl.pallas_call(
        _merge_kernel,
        grid=(TOKENS // MERGE_TM,),
        in_specs=[
            pl.BlockSpec((MERGE_TM, D_MODEL), lambda i: (i, 0)),
            pl.BlockSpec((MERGE_TM, POOL_WIDTH), lambda i: (i, 0)),
            pl.BlockSpec((MERGE_TM, Q_WIDTH), lambda i: (i, 0)),
            pl.BlockSpec((MERGE_TM, RV_WIDTH), lambda i: (i, 0)),
            pl.BlockSpec((MERGE_TM, GATE_WIDTH), lambda i: (i, 0)),
            resident(POOL_WIDTH, D_MODEL),
            resident(Q_WIDTH, D_MODEL),
            resident(RV_WIDTH, D_MODEL),
            resident(D_MODEL, D_MODEL),
            pl.BlockSpec((None, 1, D_MODEL), lambda i: (layer, 0, 0)),
        ],
        out_specs=pl.BlockSpec((MERGE_TM, D_MODEL), lambda i: (i, 0)),
        out_shape=jax.ShapeDtypeStruct((TOKENS, D_MODEL), F32),
        scratch_shapes=[pltpu.VMEM((MERGE_TM, D_MODEL), F32), pltpu.VMEM((MERGE_TM, 1), F32)],
        compiler_params=_params("parallel"),
        name="merge",
    )(h, pm, at, rt, gates, w_pool_out, w_attn_out, w_ret_out, w_out, post)


def kernel(x, ffn1_pre, ffn1_up, ffn1_down, ffn1_post, mix_pre, w_in, pool_w, pool_scale, attn_sinks, ret_norm, w_pool_out, w_attn_out, w_ret_out, w_gate, w_out, mix_post, ffn2_pre, ffn2_up, ffn2_down, ffn2_post):
    def gain(a):
        return a.reshape(DEPTH, 1, a.shape[-1])

    def bf(a):
        return a.astype(BF16)

    pool_w, w_pool_out, w_attn_out, w_ret_out, w_out = (
        bf(pool_w), bf(w_pool_out), bf(w_attn_out), bf(w_ret_out), bf(w_out))
    log_g = jnp.log(1.0 - jnp.exp2(-5.0 - jnp.arange(RET_HEADS, dtype=F32)))
    sinks = attn_sinks.reshape(DEPTH, ATTN_HEADS, 1, 1)

    h = x.reshape(TOKENS, D_MODEL)
    for l in range(DEPTH):
        h = _ffn(h, gain(ffn1_pre), ffn1_up, ffn1_down, gain(ffn1_post), l)
        p = _proj(h, gain(mix_pre), w_in, l)
        gates, rt, at, pm = _mix(h, gain(mix_pre), w_gate, p, log_g, gain(ret_norm), sinks,
                                 pool_w, gain(pool_scale), l)
        h = _merge(h, pm, at, rt, gates, w_pool_out, w_attn_out, w_ret_out, w_out, gain(mix_post), l)
        h = _ffn(h, gain(ffn2_pre), ffn2_up, ffn2_down, gain(ffn2_post), l)
    return h.reshape(BATCH, SEQ, D_MODEL)
```

```python
import jax
import jax.numpy as jnp
from jax import lax
from jax.experimental import pallas as pl
from jax.experimental.pallas import tpu as pltpu

F32 = jnp.float32
BF16 = jnp.bfloat16

D_MODEL = 2048
BATCH = 4
SEQ = 2048
DEPTH = 2
TOKENS = BATCH * SEQ

POOL_WINDOWS = (2, 4, 8, 16)
POOL_GROUP = 256
POOL_WIDTH = 1024
POOL_HALO = 16
ATTN_HEADS = 16
ATTN_KV_HEADS = 4
ATTN_GROUP = ATTN_HEADS // ATTN_KV_HEADS
ATTN_HEAD_DIM = 64
ATTN_BLOCK = 128
Q_WIDTH = 1024
KV_WIDTH = 256
RET_HEADS = 4
RET_KEY_DIM = 256
RET_VALUE_DIM = 512
RQK_WIDTH = 1024
RV_WIDTH = 2048
FFN_HIDDEN = 5632
RMS_EPS = 1e-6
GATE_WIDTH = 3 * D_MODEL

OFF_XP = 0
OFF_Q = OFF_XP + POOL_WIDTH
OFF_RQ = OFF_Q + Q_WIDTH
OFF_RK = OFF_RQ + RQK_WIDTH
OFF_RV = OFF_RK + RQK_WIDTH
OFF_RG = OFF_RV + RV_WIDTH
OFF_K = OFF_RG + RV_WIDTH
OFF_V = OFF_K + KV_WIDTH
IN_WIDTH = OFF_V + KV_WIDTH
W_IN_KV_OFFSET = POOL_WIDTH + Q_WIDTH

VMEM_LIMIT_BYTES = 60 * 1024 * 1024

FFN_TM = 1024
FFN_TF = 256
PROJ_TM = 2048
PROJ_TN = 512
MIX_TM = 1024
MIX_ROWS = ATTN_BLOCK
MIX_STEPS = MIX_TM // MIX_ROWS
GATE_TN = GATE_WIDTH // MIX_STEPS
MERGE_TM = 256
MERGE_TJ = 1024


def _params(*sem):
    return pltpu.CompilerParams(dimension_semantics=sem, vmem_limit_bytes=VMEM_LIMIT_BYTES)


def _rms_scale(x):
    return lax.rsqrt(jnp.mean(x * x, axis=-1, keepdims=True) + RMS_EPS)


def _ffn_kernel(h_ref, pre_ref, wg_ref, wu_ref, wd_ref, post_ref, o_ref, xn_ref, scale_ref):
    j = pl.program_id(1)

    def hidden_tile_step(first):
        w_gu = jnp.concatenate([wg_ref[...].astype(BF16), wu_ref[...].astype(BF16)], axis=1)
        gu = jnp.dot(xn_ref[...], w_gu, preferred_element_type=F32)
        g = gu[:, :FFN_TF]
        a = (g * jax.nn.sigmoid(g) * gu[:, FFN_TF:]).astype(BF16)
        part = jnp.dot(a, wd_ref[...].astype(BF16), preferred_element_type=F32)
        if first:
            o_ref[...] = part
        else:
            o_ref[...] += part

    @pl.when(j == 0)
    def _():
        x = h_ref[...]
        xn_ref[...] = (x * _rms_scale(x) * pre_ref[...]).astype(BF16)
        hidden_tile_step(first=True)

    @pl.when(j > 0)
    def _():
        hidden_tile_step(first=False)

    @pl.when(j == pl.num_programs(1) - 1)
    def _():
        scale_ref[...] = _rms_scale(o_ref[...])
        o_ref[...] = h_ref[...] + 0.5 * (o_ref[...] * scale_ref[...] * post_ref[...])


def _ffn(h, pre, w_up, w_down, post, layer):
    nf = FFN_HIDDEN // FFN_TF
    return pl.pallas_call(
        _ffn_kernel,
        grid=(TOKENS // FFN_TM, nf),
        in_specs=[
            pl.BlockSpec((FFN_TM, D_MODEL), lambda i, j: (i, 0)),
            pl.BlockSpec((None, 1, D_MODEL), lambda i, j: (layer, 0, 0)),
            pl.BlockSpec((None, D_MODEL, FFN_TF), lambda i, j: (layer, 0, j)),
            pl.BlockSpec((None, D_MODEL, FFN_TF), lambda i, j: (layer, 0, nf + j)),
            pl.BlockSpec((None, FFN_TF, D_MODEL), lambda i, j: (layer, j, 0)),
            pl.BlockSpec((None, 1, D_MODEL), lambda i, j: (layer, 0, 0)),
        ],
        out_specs=pl.BlockSpec((FFN_TM, D_MODEL), lambda i, j: (i, 0)),
        out_shape=jax.ShapeDtypeStruct((TOKENS, D_MODEL), F32),
        scratch_shapes=[pltpu.VMEM((FFN_TM, D_MODEL), BF16), pltpu.VMEM((FFN_TM, 1), F32)],
        compiler_params=_params("parallel", "arbitrary"),
        name="ffn",
    )(h, pre, w_up, w_up, w_down, post)


def _proj_kernel(h_ref, pre_ref, w_ref, o_ref, xn_ref):
    @pl.when(pl.program_id(1) == 0)
    def _():
        x = h_ref[...]
        xn_ref[...] = (x * _rms_scale(x) * pre_ref[...]).astype(BF16)

    o_ref[...] = jnp.dot(xn_ref[...], w_ref[...].astype(BF16),
                         preferred_element_type=F32).astype(o_ref.dtype)


def _proj(h, pre, w_in, layer):
    kv_tile = W_IN_KV_OFFSET // PROJ_TN
    last_tile = IN_WIDTH // PROJ_TN - 1

    def w_tile(j):
        return jnp.where(j < kv_tile, j, jnp.where(j < last_tile, j + 1, kv_tile))

    return pl.pallas_call(
        _proj_kernel,
        grid=(TOKENS // PROJ_TM, IN_WIDTH // PROJ_TN),
        in_specs=[
            pl.BlockSpec((PROJ_TM, D_MODEL), lambda i, j: (i, 0)),
            pl.BlockSpec((None, 1, D_MODEL), lambda i, j: (layer, 0, 0)),
            pl.BlockSpec((None, D_MODEL, PROJ_TN), lambda i, j: (layer, 0, w_tile(j))),
        ],
        out_specs=pl.BlockSpec((PROJ_TM, PROJ_TN), lambda i, j: (i, j)),
        out_shape=jax.ShapeDtypeStruct((TOKENS, IN_WIDTH), BF16),
        scratch_shapes=[pltpu.VMEM((PROJ_TM, D_MODEL), BF16)],
        compiler_params=_params("parallel", "arbitrary"),
        name="proj",
    )(h, pre, w_in)


def _pool_step(xc_ref, xp_ref, pw_ref, scale_ref, o_ref, seq_block):
    cur = xc_ref[...].astype(F32)
    prev = jnp.where(seq_block == 0, 0.0, xp_ref[...].astype(F32))
    ext = jnp.concatenate([prev, cur], axis=0)
    t = seq_block * MIX_ROWS + lax.broadcasted_iota(jnp.int32, (MIX_ROWS, 1), 0)
    for gi, w in enumerate(POOL_WINDOWS):
        cols = slice(gi * POOL_GROUP, (gi + 1) * POOL_GROUP)
        s = ext[:, cols]
        span = 1
        while span < w:
            s = s + pltpu.roll(s, span, 0)
            span *= 2
        wsum = s[POOL_HALO:, :]
        count = jnp.minimum(t + 1, w).astype(F32)
        pooled = wsum / count - cur[:, cols]
        mixed = jnp.dot(pooled.astype(BF16), pw_ref[gi], preferred_element_type=F32)
        o_ref[:, cols] = (mixed * scale_ref[:, cols]).astype(o_ref.dtype)


def _attn_fold_mask():
    ci = lax.broadcasted_iota(jnp.int32, (ATTN_BLOCK, ATTN_BLOCK), 0)
    qi = lax.broadcasted_iota(jnp.int32, (ATTN_BLOCK, ATTN_BLOCK), 1)
    return ci, qi, ci > qi


def _attn_init_bias(bias_ref):
    ci, qi, from_prev = _attn_fold_mask()
    dist = jnp.where(from_prev, qi + ATTN_BLOCK - ci, qi - ci).astype(F32)
    for h in range(ATTN_HEADS):
        alibi = -(2.0 ** (-8.0 * (h + 1) / ATTN_HEADS)) * dist
        bias_ref[0, h] = alibi
        bias_ref[1, h] = jnp.where(from_prev, -jnp.inf, alibi)


def _attn_step(sink_ref, q_ref, kc_ref, kp_ref, vc_ref, vp_ref, o_ref,
               bias_ref, s_ref, e_ref, denom_ref, seq_block):
    _, _, from_prev = _attn_fold_mask()
    seq_start = (seq_block == 0).astype(jnp.int32)
    q = q_ref[...] * (ATTN_HEAD_DIM ** -0.5)
    kk = jnp.concatenate([kp_ref[...], kc_ref[...]], axis=0)
    vv = jnp.concatenate([vp_ref[...], vc_ref[...]], axis=0)
    vv_t = vv.astype(F32).T.astype(BF16)

    for h in range(ATTN_HEADS):
        kv_cols = slice((h // ATTN_GROUP) * ATTN_HEAD_DIM, (h // ATTN_GROUP + 1) * ATTN_HEAD_DIM)
        s2 = lax.dot_general(kk[:, kv_cols], q[:, h * ATTN_HEAD_DIM:(h + 1) * ATTN_HEAD_DIM],
                             (((1,), (1,)), ((), ())), preferred_element_type=F32)
        s_ref[h] = jnp.where(from_prev, s2[:ATTN_BLOCK], s2[ATTN_BLOCK:]) + bias_ref[seq_start, h]

    s = s_ref[...]
    sink = sink_ref[...]
    m = jnp.maximum(jnp.max(s, axis=1, keepdims=True), sink)
    e = jnp.exp(s - m)
    denom_ref[...] = jnp.sum(e, axis=1, keepdims=True) + jnp.exp(sink - m)
    eb = e.astype(BF16)
    zero = jnp.zeros_like(eb)
    e_ref[:, :ATTN_BLOCK, :] = jnp.where(from_prev, eb, zero)
    e_ref[:, ATTN_BLOCK:, :] = jnp.where(from_prev, zero, eb)

    for pair in range(ATTN_HEADS // 2):
        outs = []
        for h in (2 * pair, 2 * pair + 1):
            kv_rows = slice((h // ATTN_GROUP) * ATTN_HEAD_DIM, (h // ATTN_GROUP + 1) * ATTN_HEAD_DIM)
            pv_t = jnp.dot(vv_t[kv_rows], e_ref[h], preferred_element_type=F32)
            outs.append(pv_t / denom_ref[h])
        pair_cols = slice(2 * pair * ATTN_HEAD_DIM, (2 * pair + 2) * ATTN_HEAD_DIM)
        o_ref[:, pair_cols] = jnp.concatenate(outs, axis=0).T.astype(o_ref.dtype)


def _ret_init_tables(logg_ref, intra_ref, qdec_ref, kdec_ref):
    key_scale = RET_KEY_DIM ** -0.5
    pos = lax.broadcasted_iota(jnp.int32, (MIX_ROWS, 1), 0).astype(F32)
    row = lax.broadcasted_iota(jnp.int32, (MIX_ROWS, MIX_ROWS), 0)
    col = lax.broadcasted_iota(jnp.int32, (MIX_ROWS, MIX_ROWS), 1)
    diff = (row - col).astype(F32)
    for h in range(RET_HEADS):
        lg = logg_ref[h]
        intra_ref[h] = jnp.where(diff >= 0, jnp.exp(lg * jnp.maximum(diff, 0.0)), 0.0) * key_scale
        qdec_ref[h] = jnp.exp(lg * (pos + 1.0))
        kdec_ref[h] = jnp.exp(lg * (MIX_ROWS - 1.0 - pos)) * key_scale


def _ret_step(logg_ref, q_ref, k_ref, v_ref, g_ref, norm_ref, o_ref,
              state_ref, intra_ref, qdec_ref, kdec_ref):
    for h in range(RET_HEADS):
        k_cols = slice(h * RET_KEY_DIM, (h + 1) * RET_KEY_DIM)
        v_cols = slice(h * RET_VALUE_DIM, (h + 1) * RET_VALUE_DIM)
        chunk_decay = jnp.exp(jnp.full((1, 1), MIX_ROWS, F32) * logg_ref[h])
        q = q_ref[:, k_cols]
        k = k_ref[:, k_cols]
        v = v_ref[:, v_cols]
        state = state_ref[h]
        inner = lax.dot_general(q, k, (((1,), (1,)), ((), ())), preferred_element_type=F32) * intra_ref[h]
        qd = (q.astype(F32) * qdec_ref[h]).astype(BF16)
        o = (jnp.dot(inner.astype(BF16), v, preferred_element_type=F32)
             + jnp.dot(qd, state.astype(BF16), preferred_element_type=F32))
        kd = (k.astype(F32) * kdec_ref[h]).astype(BF16)
        state_ref[h] = state * chunk_decay + lax.dot_general(
            kd, v, (((0,), (0,)), ((), ())), preferred_element_type=F32)

        o = o * _rms_scale(o) * norm_ref[:, v_cols]
        g = g_ref[:, v_cols].astype(F32)
        o_ref[:, v_cols] = (g * jax.nn.sigmoid(g) * o).astype(o_ref.dtype)


def _mix_kernel(logg_ref, h_ref, pre_ref, wgate_ref,
                rq_ref, rk_ref, rv_ref, rg_ref, norm_ref,
                sink_ref, aq_ref, kc_ref, kp_ref, vc_ref, vp_ref,
                xc_ref, xp_ref, pw_ref, pscale_ref,
                gates_ref, rt_ref, at_ref, pm_ref,
                xn_ref, state_ref, intra_ref, qdec_ref, kdec_ref,
                bias_ref, s_ref, e_ref, denom_ref):
    i = pl.program_id(0)
    j = pl.program_id(1)
    seq_block = (i * MIX_STEPS + j) % (SEQ // MIX_ROWS)

    @pl.when((i == 0) & (j == 0))
    def _():
        _ret_init_tables(logg_ref, intra_ref, qdec_ref, kdec_ref)
        _attn_init_bias(bias_ref)

    @pl.when(seq_block == 0)
    def _():
        state_ref[...] = jnp.zeros_like(state_ref)

    def step():
        gates_ref[...] = jnp.dot(xn_ref[...], wgate_ref[...].astype(BF16),
                                 preferred_element_type=F32).astype(gates_ref.dtype)
        _ret_step(logg_ref, rq_ref, rk_ref, rv_ref, rg_ref, norm_ref, rt_ref,
                  state_ref, intra_ref, qdec_ref, kdec_ref)
        _attn_step(sink_ref, aq_ref, kc_ref, kp_ref, vc_ref, vp_ref, at_ref,
                   bias_ref, s_ref, e_ref, denom_ref, seq_block)
        _pool_step(xc_ref, xp_ref, pw_ref, pscale_ref, pm_ref, seq_block)

    @pl.when(j == 0)
    def _():
        x = h_ref[...]
        xn_ref[...] = (x * _rms_scale(x) * pre_ref[...]).astype(BF16)
        step()

    pl.when(j > 0)(step)


def _mix(h, pre, w_gate, p, log_g, ret_norm, sinks, pool_w, pool_scale, layer):
    def row(i, j):
        return i * MIX_STEPS + j

    def prev_row(i, j):
        return jnp.maximum(row(i, j) - 1, 0)

    def slab(width, offset):
        return pl.BlockSpec((MIX_ROWS, width), lambda i, j: (row(i, j), offset // width))

    halo_blocks = MIX_ROWS // POOL_HALO
    kcol = OFF_K // KV_WIDTH
    vcol = OFF_V // KV_WIDTH
    in_specs = [
        pl.BlockSpec(memory_space=pltpu.SMEM),
        pl.BlockSpec((MIX_TM, D_MODEL), lambda i, j: (i, 0)),
        pl.BlockSpec((None, 1, D_MODEL), lambda i, j: (layer, 0, 0)),
        pl.BlockSpec((None, D_MODEL, GATE_TN), lambda i, j: (layer, 0, j)),
        slab(RQK_WIDTH, OFF_RQ), slab(RQK_WIDTH, OFF_RK), slab(RV_WIDTH, OFF_RV), slab(RV_WIDTH, OFF_RG),
        pl.BlockSpec((None, 1, RV_WIDTH), lambda i, j: (layer, 0, 0)),
        pl.BlockSpec((None, ATTN_HEADS, 1, 1), lambda i, j: (layer, 0, 0, 0)),
        slab(Q_WIDTH, OFF_Q),
        pl.BlockSpec((MIX_ROWS, KV_WIDTH), lambda i, j: (row(i, j), kcol)),
        pl.BlockSpec((MIX_ROWS, KV_WIDTH), lambda i, j: (prev_row(i, j), kcol)),
        pl.BlockSpec((MIX_ROWS, KV_WIDTH), lambda i, j: (row(i, j), vcol)),
        pl.BlockSpec((MIX_ROWS, KV_WIDTH), lambda i, j: (prev_row(i, j), vcol)),
        slab(POOL_WIDTH, OFF_XP),
        pl.BlockSpec((POOL_HALO, POOL_WIDTH),
                     lambda i, j: (jnp.maximum(row(i, j) * halo_blocks - 1, 0), 0)),
        pl.BlockSpec((None, len(POOL_WINDOWS), POOL_GROUP, POOL_GROUP), lambda i, j: (layer, 0, 0, 0)),
        pl.BlockSpec((None, 1, POOL_WIDTH), lambda i, j: (layer, 0, 0)),
    ]
    return pl.pallas_call(
        _mix_kernel,
        grid=(TOKENS // MIX_TM, MIX_STEPS),
        in_specs=in_specs,
        out_specs=[
            pl.BlockSpec((MIX_TM, GATE_TN), lambda i, j: (i, j)),
            pl.BlockSpec((MIX_ROWS, RV_WIDTH), lambda i, j: (row(i, j), 0)),
            pl.BlockSpec((MIX_ROWS, Q_WIDTH), lambda i, j: (row(i, j), 0)),
            pl.BlockSpec((MIX_ROWS, POOL_WIDTH), lambda i, j: (row(i, j), 0)),
        ],
        out_shape=[
            jax.ShapeDtypeStruct((TOKENS, GATE_WIDTH), BF16),
            jax.ShapeDtypeStruct((TOKENS, RV_WIDTH), BF16),
            jax.ShapeDtypeStruct((TOKENS, Q_WIDTH), BF16),
            jax.ShapeDtypeStruct((TOKENS, POOL_WIDTH), BF16),
        ],
        scratch_shapes=[
            pltpu.VMEM((MIX_TM, D_MODEL), BF16),
            pltpu.VMEM((RET_HEADS, RET_KEY_DIM, RET_VALUE_DIM), F32),
            pltpu.VMEM((RET_HEADS, MIX_ROWS, MIX_ROWS), F32),
            pltpu.VMEM((RET_HEADS, MIX_ROWS, 1), F32),
            pltpu.VMEM((RET_HEADS, MIX_ROWS, 1), F32),
            pltpu.VMEM((2, ATTN_HEADS, ATTN_BLOCK, ATTN_BLOCK), F32),
            pltpu.VMEM((ATTN_HEADS, ATTN_BLOCK, ATTN_BLOCK), F32),
            pltpu.VMEM((ATTN_HEADS, 2 * ATTN_BLOCK, ATTN_BLOCK), BF16),
            pltpu.VMEM((ATTN_HEADS, 1, ATTN_BLOCK), F32),
        ],
        compiler_params=_params("arbitrary", "arbitrary"),
        name="mix",
    )(log_g, h, pre, w_gate, p, p, p, p, ret_norm, sinks,
      p, p, p, p, p, p, p, pool_w, pool_scale)


def _merge_kernel(h_ref, pm_ref, at_ref, rt_ref, gates_ref,
                  wp_ref, wa_ref, wr_ref, wo_ref, post_ref, o_ref, acc_ref, scale_ref):
    def gate(branch, cols):
        return jax.nn.sigmoid(gates_ref[:, branch * D_MODEL + cols.start:branch * D_MODEL + cols.stop]
                              .astype(F32))

    for jt in range(D_MODEL // MERGE_TJ):
        cols = slice(jt * MERGE_TJ, (jt + 1) * MERGE_TJ)
        merged = (gate(0, cols) * jnp.dot(pm_ref[...], wp_ref[:, cols], preferred_element_type=F32)
                  + gate(1, cols) * jnp.dot(at_ref[...], wa_ref[:, cols], preferred_element_type=F32)
                  + gate(2, cols) * jnp.dot(rt_ref[...], wr_ref[:, cols], preferred_element_type=F32))
        part = jnp.dot(merged.astype(BF16), wo_ref[cols, :], preferred_element_type=F32)
        if jt == 0:
            acc_ref[...] = part
        else:
            acc_ref[...] += part

    scale_ref[...] = _rms_scale(acc_ref[...])
    o_ref[...] = h_ref[...] + acc_ref[...] * scale_ref[...] * post_ref[...]


def _merge(h, pm, at, rt, gates, w_pool_out, w_attn_out, w_ret_out, w_out, post, layer):
    def resident(rows, cols):
        return pl.BlockSpec((None, rows, cols), lambda i: (layer, 0, 0), pipeline_mode=pl.Buffered(1))

    return pl.pallas_call(
        _merge_kernel,
        grid=(TOKENS // MERGE_TM,),
        in_specs=[
            pl.BlockSpec((MERGE_TM, D_MODEL), lambda i: (i, 0)),
            pl.BlockSpec((MERGE_TM, POOL_WIDTH), lambda i: (i, 0)),
            pl.BlockSpec((MERGE_TM, Q_WIDTH), lambda i: (i, 0)),
            pl.BlockSpec((MERGE_TM, RV_WIDTH), lambda i: (i, 0)),
            pl.BlockSpec((MERGE_TM, GATE_WIDTH), lambda i: (i, 0)),
            resident(POOL_WIDTH, D_MODEL),
            resident(Q_WIDTH, D_MODEL),
            resident(RV_WIDTH, D_MODEL),
            resident(D_MODEL, D_MODEL),
            pl.BlockSpec((None, 1, D_MODEL), lambda i: (layer, 0, 0)),
        ],
        out_specs=pl.BlockSpec((MERGE_TM, D_MODEL), lambda i: (i, 0)),
        out_shape=jax.ShapeDtypeStruct((TOKENS, D_MODEL), F32),
        scratch_shapes=[pltpu.VMEM((MERGE_TM, D_MODEL), F32), pltpu.VMEM((MERGE_TM, 1), F32)],
        compiler_params=_params("parallel"),
        name="merge",
    )(h, pm, at, rt, gates, w_pool_out, w_attn_out, w_ret_out, w_out, post)


def kernel(x, ffn1_pre, ffn1_up, ffn1_down, ffn1_post, mix_pre, w_in, pool_w, pool_scale, attn_sinks, ret_norm, w_pool_out, w_attn_out, w_ret_out, w_gate, w_out, mix_post, ffn2_pre, ffn2_up, ffn2_down, ffn2_post):
    def gain(a):
        return a.reshape(DEPTH, 1, a.shape[-1])

    def bf(a):
        return a.astype(BF16)

    pool_w, w_pool_out, w_attn_out, w_ret_out, w_out = (
        bf(pool_w), bf(w_pool_out), bf(w_attn_out), bf(w_ret_out), bf(w_out))
    log_g = jnp.log(1.0 - jnp.exp2(-5.0 - jnp.arange(RET_HEADS, dtype=F32)))
    sinks = attn_sinks.reshape(DEPTH, ATTN_HEADS, 1, 1)

    h = x.reshape(TOKENS, D_MODEL)
    for l in range(DEPTH):
        h = _ffn(h, gain(ffn1_pre), ffn1_up, ffn1_down, gain(ffn1_post), l)
        p = _proj(h, gain(mix_pre), w_in, l)
        gates, rt, at, pm = _mix(h, gain(mix_pre), w_gate, p, log_g, gain(ret_norm), sinks,
                                 pool_w, gain(pool_scale), l)
        h = _merge(h, pm, at, rt, gates, w_pool_out, w_attn_out, w_ret_out, w_out, gain(mix_post), l)
        h = _ffn(h, gain(ffn2_pre), ffn2_up, ffn2_down, gain(ffn2_post), l)
    return h.reshape(BATCH, SEQ, D_MODEL)
```

```python
import jax
import jax.numpy as jnp
from jax import lax
from jax.experimental import pallas as pl
from jax.experimental.pallas import tpu as pltpu

F32 = jnp.float32
BF16 = jnp.bfloat16

D_MODEL = 2048
BATCH = 4
SEQ = 2048
DEPTH = 2
TOKENS = BATCH * SEQ

POOL_WINDOWS = (2, 4, 8, 16)
POOL_GROUP = 256
POOL_WIDTH = 1024
POOL_HALO = 16
ATTN_HEADS = 16
ATTN_KV_HEADS = 4
ATTN_GROUP = ATTN_HEADS // ATTN_KV_HEADS
ATTN_HEAD_DIM = 64
ATTN_BLOCK = 128
Q_WIDTH = 1024
KV_WIDTH = 256
RET_HEADS = 4
RET_KEY_DIM = 256
RET_VALUE_DIM = 512
RQK_WIDTH = 1024
RV_WIDTH = 2048
FFN_HIDDEN = 5632
RMS_EPS = 1e-6
GATE_WIDTH = 3 * D_MODEL

OFF_XP = 0
OFF_Q = OFF_XP + POOL_WIDTH
OFF_RQ = OFF_Q + Q_WIDTH
OFF_RK = OFF_RQ + RQK_WIDTH
OFF_RV = OFF_RK + RQK_WIDTH
OFF_RG = OFF_RV + RV_WIDTH
OFF_K = OFF_RG + RV_WIDTH
OFF_V = OFF_K + KV_WIDTH
IN_WIDTH = OFF_V + KV_WIDTH
W_IN_KV_OFFSET = POOL_WIDTH + Q_WIDTH

VMEM_LIMIT_BYTES = 60 * 1024 * 1024

FFN_TM = 1024
FFN_TF = 256
FFN_TAIL_TF = 512
PROJ_TM = 2048
PROJ_TN = 512
MIX_TM = 1024
MIX_ROWS = ATTN_BLOCK
MIX_STEPS = MIX_TM // MIX_ROWS
GATE_TN = GATE_WIDTH // MIX_STEPS
MERGE_TM = 256
MERGE_TJ = 1024


def _params(*sem):
    return pltpu.CompilerParams(dimension_semantics=sem, vmem_limit_bytes=VMEM_LIMIT_BYTES)


def _rms_scale(x):
    return lax.rsqrt(jnp.mean(x * x, axis=-1, keepdims=True) + RMS_EPS)


def _ffn_steps(h_ref, pre_ref, wg_ref, wu_ref, wd_ref, post_ref, o_ref, xn_ref, scale_ref, fused_gate_up):
    j = pl.program_id(1)

    def hidden_tile_step(first):
        if fused_gate_up:
            tf = wg_ref.shape[-1]
            gu = jnp.dot(xn_ref[...], jnp.concatenate([wg_ref[...], wu_ref[...]], axis=1),
                         preferred_element_type=F32)
            g, u = gu[:, :tf], gu[:, tf:]
        else:
            g = jnp.dot(xn_ref[...], wg_ref[...], preferred_element_type=F32)
            u = jnp.dot(xn_ref[...], wu_ref[...], preferred_element_type=F32)
        a = (g * jax.nn.sigmoid(g) * u).astype(BF16)
        part = jnp.dot(a, wd_ref[...], preferred_element_type=F32)
        if first:
            o_ref[...] = part
        else:
            o_ref[...] += part

    @pl.when(j == 0)
    def _():
        x = h_ref[...]
        xn_ref[...] = (x * _rms_scale(x) * pre_ref[...]).astype(BF16)
        hidden_tile_step(first=True)

    @pl.when(j > 0)
    def _():
        hidden_tile_step(first=False)

    @pl.when(j == pl.num_programs(1) - 1)
    def _():
        scale_ref[...] = _rms_scale(o_ref[...])
        o_ref[...] = h_ref[...] + 0.5 * (o_ref[...] * scale_ref[...] * post_ref[...])


def _ffn_head_kernel(h_ref, pre_ref, wg_ref, wu_ref, wd_ref, post_ref,
                     o_ref, wg_bf_ref, wu_bf_ref, wd_bf_ref, xn_ref, scale_ref):
    wg_bf_ref[...] = wg_ref[...].astype(BF16)
    wu_bf_ref[...] = wu_ref[...].astype(BF16)
    wd_bf_ref[...] = wd_ref[...].astype(BF16)
    _ffn_steps(h_ref, pre_ref, wg_bf_ref, wu_bf_ref, wd_bf_ref, post_ref, o_ref, xn_ref, scale_ref,
               fused_gate_up=True)


def _ffn_tail_kernel(head_out_ref, h_ref, pre_ref, wg_ref, wu_ref, wd_ref, post_ref, o_ref, xn_ref, scale_ref):
    del head_out_ref
    _ffn_steps(h_ref, pre_ref, wg_ref, wu_ref, wd_ref, post_ref, o_ref, xn_ref, scale_ref,
               fused_gate_up=False)


def _ffn(h, pre, w_up, w_down, post, layer):
    nf = FFN_HIDDEN // FFN_TF
    gain_spec = pl.BlockSpec((None, 1, D_MODEL), lambda i, j: (layer, 0, 0))
    scratch = [pltpu.VMEM((FFN_TM, D_MODEL), BF16), pltpu.VMEM((FFN_TM, 1), F32)]
    out, wg_bf, wu_bf, wd_bf = pl.pallas_call(
        _ffn_head_kernel,
        grid=(1, nf),
        in_specs=[
            pl.BlockSpec((FFN_TM, D_MODEL), lambda i, j: (0, 0)),
            gain_spec,
            pl.BlockSpec((None, D_MODEL, FFN_TF), lambda i, j: (layer, 0, j)),
            pl.BlockSpec((None, D_MODEL, FFN_TF), lambda i, j: (layer, 0, nf + j)),
            pl.BlockSpec((None, FFN_TF, D_MODEL), lambda i, j: (layer, j, 0)),
            gain_spec,
        ],
        out_specs=[
            pl.BlockSpec((FFN_TM, D_MODEL), lambda i, j: (0, 0)),
            pl.BlockSpec((D_MODEL, FFN_TF), lambda i, j: (0, j)),
            pl.BlockSpec((D_MODEL, FFN_TF), lambda i, j: (0, j)),
            pl.BlockSpec((FFN_TF, D_MODEL), lambda i, j: (j, 0)),
        ],
        out_shape=[
            jax.ShapeDtypeStruct((TOKENS, D_MODEL), F32),
            jax.ShapeDtypeStruct((D_MODEL, FFN_HIDDEN), BF16),
            jax.ShapeDtypeStruct((D_MODEL, FFN_HIDDEN), BF16),
            jax.ShapeDtypeStruct((FFN_HIDDEN, D_MODEL), BF16),
        ],
        scratch_shapes=scratch,
        compiler_params=_params("arbitrary", "arbitrary"),
        name="ffn_head",
    )(h, pre, w_up, w_up, w_down, post)

    nf2 = FFN_HIDDEN // FFN_TAIL_TF
    return pl.pallas_call(
        _ffn_tail_kernel,
        grid=(TOKENS // FFN_TM - 1, nf2),
        in_specs=[
            pl.BlockSpec(memory_space=pl.ANY),
            pl.BlockSpec((FFN_TM, D_MODEL), lambda i, j: (i + 1, 0)),
            gain_spec,
            pl.BlockSpec((D_MODEL, FFN_TAIL_TF), lambda i, j: (0, j)),
            pl.BlockSpec((D_MODEL, FFN_TAIL_TF), lambda i, j: (0, j)),
            pl.BlockSpec((FFN_TAIL_TF, D_MODEL), lambda i, j: (j, 0)),
            gain_spec,
        ],
        out_specs=pl.BlockSpec((FFN_TM, D_MODEL), lambda i, j: (i + 1, 0)),
        out_shape=jax.ShapeDtypeStruct((TOKENS, D_MODEL), F32),
        input_output_aliases={0: 0},
        scratch_shapes=scratch,
        compiler_params=_params("parallel", "arbitrary"),
        name="ffn_tail",
    )(out, h, pre, wg_bf, wu_bf, wd_bf, post)


def _proj_kernel(h_ref, pre_ref, w_ref, o_ref, xn_ref):
    @pl.when(pl.program_id(1) == 0)
    def _():
        x = h_ref[...]
        xn_ref[...] = (x * _rms_scale(x) * pre_ref[...]).astype(BF16)

    o_ref[...] = jnp.dot(xn_ref[...], w_ref[...].astype(BF16),
                         preferred_element_type=F32).astype(o_ref.dtype)


def _proj(h, pre, w_in, layer):
    kv_tile = W_IN_KV_OFFSET // PROJ_TN
    last_tile = IN_WIDTH // PROJ_TN - 1

    def w_tile(j):
        return jnp.where(j < kv_tile, j, jnp.where(j < last_tile, j + 1, kv_tile))

    return pl.pallas_call(
        _proj_kernel,
        grid=(TOKENS // PROJ_TM, IN_WIDTH // PROJ_TN),
        in_specs=[
            pl.BlockSpec((PROJ_TM, D_MODEL), lambda i, j: (i, 0)),
            pl.BlockSpec((None, 1, D_MODEL), lambda i, j: (layer, 0, 0)),
            pl.BlockSpec((None, D_MODEL, PROJ_TN), lambda i, j: (layer, 0, w_tile(j))),
        ],
        out_specs=pl.BlockSpec((PROJ_TM, PROJ_TN), lambda i, j: (i, j)),
        out_shape=jax.ShapeDtypeStruct((TOKENS, IN_WIDTH), BF16),
        scratch_shapes=[pltpu.VMEM((PROJ_TM, D_MODEL), BF16)],
        compiler_params=_params("parallel", "arbitrary"),
        name="proj",
    )(h, pre, w_in)


def _pool_step(xc_ref, xp_ref, pw_ref, scale_ref, o_ref, seq_block):
    cur = xc_ref[...].astype(F32)
    prev = jnp.where(seq_block == 0, 0.0, xp_ref[...].astype(F32))
    ext = jnp.concatenate([prev, cur], axis=0)
    t = seq_block * MIX_ROWS + lax.broadcasted_iota(jnp.int32, (MIX_ROWS, 1), 0)
    for gi, w in enumerate(POOL_WINDOWS):
        cols = slice(gi * POOL_GROUP, (gi + 1) * POOL_GROUP)
        s = ext[:, cols]
        span = 1
        while span < w:
            s = s + pltpu.roll(s, span, 0)
            span *= 2
        wsum = s[POOL_HALO:, :]
        count = jnp.minimum(t + 1, w).astype(F32)
        pooled = wsum / count - cur[:, cols]
        mixed = jnp.dot(pooled.astype(BF16), pw_ref[gi], preferred_element_type=F32)
        o_ref[:, cols] = (mixed * scale_ref[:, cols]).astype(o_ref.dtype)


def _attn_fold_mask():
    ci = lax.broadcasted_iota(jnp.int32, (ATTN_BLOCK, ATTN_BLOCK), 0)
    qi = lax.broadcasted_iota(jnp.int32, (ATTN_BLOCK, ATTN_BLOCK), 1)
    return ci, qi, ci > qi


def _attn_init_bias(bias_ref):
    ci, qi, from_prev = _attn_fold_mask()
    dist = jnp.where(from_prev, qi + ATTN_BLOCK - ci, qi - ci).astype(F32)
    for h in range(ATTN_HEADS):
        alibi = -(2.0 ** (-8.0 * (h + 1) / ATTN_HEADS)) * dist
        bias_ref[0, h] = alibi
        bias_ref[1, h] = jnp.where(from_prev, -jnp.inf, alibi)


def _attn_step(sink_ref, q_ref, kc_ref, kp_ref, vc_ref, vp_ref, o_ref,
               bias_ref, s_ref, e_ref, denom_ref, seq_block):
    _, _, from_prev = _attn_fold_mask()
    seq_start = (seq_block == 0).astype(jnp.int32)
    q = q_ref[...] * (ATTN_HEAD_DIM ** -0.5)
    kk = jnp.concatenate([kp_ref[...], kc_ref[...]], axis=0)
    vv = jnp.concatenate([vp_ref[...], vc_ref[...]], axis=0)
    vv_t = vv.astype(F32).T.astype(BF16)

    for h in range(ATTN_HEADS):
        kv_cols = slice((h // ATTN_GROUP) * ATTN_HEAD_DIM, (h // ATTN_GROUP + 1) * ATTN_HEAD_DIM)
        s2 = lax.dot_general(kk[:, kv_cols], q[:, h * ATTN_HEAD_DIM:(h + 1) * ATTN_HEAD_DIM],
                             (((1,), (1,)), ((), ())), preferred_element_type=F32)
        s_ref[h] = jnp.where(from_prev, s2[:ATTN_BLOCK], s2[ATTN_BLOCK:]) + bias_ref[seq_start, h]

    s = s_ref[...]
    sink = sink_ref[...]
    m = jnp.maximum(jnp.max(s, axis=1, keepdims=True), sink)
    e = jnp.exp(s - m)
    denom_ref[...] = jnp.sum(e, axis=1, keepdims=True) + jnp.exp(sink - m)
    eb = e.astype(BF16)
    zero = jnp.zeros_like(eb)
    e_ref[:, :ATTN_BLOCK, :] = jnp.where(from_prev, eb, zero)
    e_ref[:, ATTN_BLOCK:, :] = jnp.where(from_prev, zero, eb)

    for pair in range(ATTN_HEADS // 2):
        outs = []
        for h in (2 * pair, 2 * pair + 1):
            kv_rows = slice((h // ATTN_GROUP) * ATTN_HEAD_DIM, (h // ATTN_GROUP + 1) * ATTN_HEAD_DIM)
            pv_t = jnp.dot(vv_t[kv_rows], e_ref[h], preferred_element_type=F32)
            outs.append(pv_t / denom_ref[h])
        pair_cols = slice(2 * pair * ATTN_HEAD_DIM, (2 * pair + 2) * ATTN_HEAD_DIM)
        o_ref[:, pair_cols] = jnp.concatenate(outs, axis=0).T.astype(o_ref.dtype)


def _ret_init_tables(logg_ref, intra_ref, qdec_ref, kdec_ref):
    key_scale = RET_KEY_DIM ** -0.5
    pos = lax.broadcasted_iota(jnp.int32, (MIX_ROWS, 1), 0).astype(F32)
    row = lax.broadcasted_iota(jnp.int32, (MIX_ROWS, MIX_ROWS), 0)
    col = lax.broadcasted_iota(jnp.int32, (MIX_ROWS, MIX_ROWS), 1)
    diff = (row - col).astype(F32)
    for h in range(RET_HEADS):
        lg = logg_ref[h]
        intra_ref[h] = jnp.where(diff >= 0, jnp.exp(lg * jnp.maximum(diff, 0.0)), 0.0) * key_scale
        qdec_ref[h] = jnp.exp(lg * (pos + 1.0))
        kdec_ref[h] = jnp.exp(lg * (MIX_ROWS - 1.0 - pos)) * key_scale


def _ret_step(logg_ref, q_ref, k_ref, v_ref, g_ref, norm_ref, o_ref,
              state_ref, intra_ref, qdec_ref, kdec_ref):
    for h in range(RET_HEADS):
        k_cols = slice(h * RET_KEY_DIM, (h + 1) * RET_KEY_DIM)
        v_cols = slice(h * RET_VALUE_DIM, (h + 1) * RET_VALUE_DIM)
        chunk_decay = jnp.exp(jnp.full((1, 1), MIX_ROWS, F32) * logg_ref[h])
        q = q_ref[:, k_cols]
        k = k_ref[:, k_cols]
        v = v_ref[:, v_cols]
        state = state_ref[h]
        inner = lax.dot_general(q, k, (((1,), (1,)), ((), ())), preferred_element_type=F32) * intra_ref[h]
        qd = (q.astype(F32) * qdec_ref[h]).astype(BF16)
        o = (jnp.dot(inner.astype(BF16), v, preferred_element_type=F32)
             + jnp.dot(qd, state.astype(BF16), preferred_element_type=F32))
        kd = (k.astype(F32) * kdec_ref[h]).astype(BF16)
        state_ref[h] = state * chunk_decay + lax.dot_general(
            kd, v, (((0,), (0,)), ((), ())), preferred_element_type=F32)

        o = o * _rms_scale(o) * norm_ref[:, v_cols]
        g = g_ref[:, v_cols].astype(F32)
        o_ref[:, v_cols] = (g * jax.nn.sigmoid(g) * o).astype(o_ref.dtype)


def _mix_kernel(logg_ref, h_ref, pre_ref, wgate_ref,
                rq_ref, rk_ref, rv_ref, rg_ref, norm_ref,
                sink_ref, aq_ref, kc_ref, kp_ref, vc_ref, vp_ref,
                xc_ref, xp_ref, pw_ref, pscale_ref,
                gates_ref, rt_ref, at_ref, pm_ref,
                xn_ref, state_ref, intra_ref, qdec_ref, kdec_ref,
                bias_ref, s_ref, e_ref, denom_ref):
    i = pl.program_id(0)
    j = pl.program_id(1)
    seq_block = (i * MIX_STEPS + j) % (SEQ // MIX_ROWS)

    @pl.when((i == 0) & (j == 0))
    def _():
        _ret_init_tables(logg_ref, intra_ref, qdec_ref, kdec_ref)
        _attn_init_bias(bias_ref)

    @pl.when(seq_block == 0)
    def _():
        state_ref[...] = jnp.zeros_like(state_ref)

    def step():
        gates_ref[...] = jnp.dot(xn_ref[...], wgate_ref[...].astype(BF16),
                                 preferred_element_type=F32).astype(gates_ref.dtype)
        _ret_step(logg_ref, rq_ref, rk_ref, rv_ref, rg_ref, norm_ref, rt_ref,
                  state_ref, intra_ref, qdec_ref, kdec_ref)
        _attn_step(sink_ref, aq_ref, kc_ref, kp_ref, vc_ref, vp_ref, at_ref,
                   bias_ref, s_ref, e_ref, denom_ref, seq_block)
        _pool_step(xc_ref, xp_ref, pw_ref, pscale_ref, pm_ref, seq_block)

    @pl.when(j == 0)
    def _():
        x = h_ref[...]
        xn_ref[...] = (x * _rms_scale(x) * pre_ref[...]).astype(BF16)
        step()

    pl.when(j > 0)(step)


def _mix(h, pre, w_gate, p, log_g, ret_norm, sinks, pool_w, pool_scale, layer):
    def row(i, j):
        return i * MIX_STEPS + j

    def prev_row(i, j):
        return jnp.maximum(row(i, j) - 1, 0)

    def slab(width, offset):
        return pl.BlockSpec((MIX_ROWS, width), lambda i, j: (row(i, j), offset // width))

    halo_blocks = MIX_ROWS // POOL_HALO
    kcol = OFF_K // KV_WIDTH
    vcol = OFF_V // KV_WIDTH
    in_specs = [
        pl.BlockSpec(memory_space=pltpu.SMEM),
        pl.BlockSpec((MIX_TM, D_MODEL), lambda i, j: (i, 0)),
        pl.BlockSpec((None, 1, D_MODEL), lambda i, j: (layer, 0, 0)),
        pl.BlockSpec((None, D_MODEL, GATE_TN), lambda i, j: (layer, 0, j)),
        slab(RQK_WIDTH, OFF_RQ), slab(RQK_WIDTH, OFF_RK), slab(RV_WIDTH, OFF_RV), slab(RV_WIDTH, OFF_RG),
        pl.BlockSpec((None, 1, RV_WIDTH), lambda i, j: (layer, 0, 0)),
        pl.BlockSpec((None, ATTN_HEADS, 1, 1), lambda i, j: (layer, 0, 0, 0)),
        slab(Q_WIDTH, OFF_Q),
        pl.BlockSpec((MIX_ROWS, KV_WIDTH), lambda i, j: (row(i, j), kcol)),
        pl.BlockSpec((MIX_ROWS, KV_WIDTH), lambda i, j: (prev_row(i, j), kcol)),
        pl.BlockSpec((MIX_ROWS, KV_WIDTH), lambda i, j: (row(i, j), vcol)),
        pl.BlockSpec((MIX_ROWS, KV_WIDTH), lambda i, j: (prev_row(i, j), vcol)),
        slab(POOL_WIDTH, OFF_XP),
        pl.BlockSpec((POOL_HALO, POOL_WIDTH),
                     lambda i, j: (jnp.maximum(row(i, j) * halo_blocks - 1, 0), 0)),
        pl.BlockSpec((None, len(POOL_WINDOWS), POOL_GROUP, POOL_GROUP), lambda i, j: (layer, 0, 0, 0)),
        pl.BlockSpec((None, 1, POOL_WIDTH), lambda i, j: (layer, 0, 0)),
    ]
    return pl.pallas_call(
        _mix_kernel,
        grid=(TOKENS // MIX_TM, MIX_STEPS),
        in_specs=in_specs,
        out_specs=[
            pl.BlockSpec((MIX_TM, GATE_TN), lambda i, j: (i, j)),
            pl.BlockSpec((MIX_ROWS, RV_WIDTH), lambda i, j: (row(i, j), 0)),
            pl.BlockSpec((MIX_ROWS, Q_WIDTH), lambda i, j: (row(i, j), 0)),
            pl.BlockSpec((MIX_ROWS, POOL_WIDTH), lambda i, j: (row(i, j), 0)),
        ],
        out_shape=[
            jax.ShapeDtypeStruct((TOKENS, GATE_WIDTH), BF16),
            jax.ShapeDtypeStruct((TOKENS, RV_WIDTH), BF16),
            jax.ShapeDtypeStruct((TOKENS, Q_WIDTH), BF16),
            jax.ShapeDtypeStruct((TOKENS, POOL_WIDTH), BF16),
        ],
        scratch_shapes=[
            pltpu.VMEM((MIX_TM, D_MODEL), BF16),
            pltpu.VMEM((RET_HEADS, RET_KEY_DIM, RET_VALUE_DIM), F32),
            pltpu.VMEM((RET_HEADS, MIX_ROWS, MIX_ROWS), F32),
            pltpu.VMEM((RET_HEADS, MIX_ROWS, 1), F32),
            pltpu.VMEM((RET_HEADS, MIX_ROWS, 1), F32),
            pltpu.VMEM((2, ATTN_HEADS, ATTN_BLOCK, ATTN_BLOCK), F32),
            pltpu.VMEM((ATTN_HEADS, ATTN_BLOCK, ATTN_BLOCK), F32),
            pltpu.VMEM((ATTN_HEADS, 2 * ATTN_BLOCK, ATTN_BLOCK), BF16),
            pltpu.VMEM((ATTN_HEADS, 1, ATTN_BLOCK), F32),
        ],
        compiler_params=_params("arbitrary", "arbitrary"),
        name="mix",
    )(log_g, h, pre, w_gate, p, p, p, p, ret_norm, sinks,
      p, p, p, p, p, p, p, pool_w, pool_scale)


def _merge_kernel(h_ref, pm_ref, at_ref, rt_ref, gates_ref,
                  wp_ref, wa_ref, wr_ref, wo_ref, post_ref, o_ref, acc_ref, scale_ref):
    def gate(branch, cols):
        return jax.nn.sigmoid(gates_ref[:, branch * D_MODEL + cols.start:branch * D_MODEL + cols.stop]
                              .astype(F32))

    for jt in range(D_MODEL // MERGE_TJ):
        cols = slice(jt * MERGE_TJ, (jt + 1) * MERGE_TJ)
        merged = (gate(0, cols) * jnp.dot(pm_ref[...], wp_ref[:, cols], preferred_element_type=F32)
                  + gate(1, cols) * jnp.dot(at_ref[...], wa_ref[:, cols], preferred_element_type=F32)
                  + gate(2, cols) * jnp.dot(rt_ref[...], wr_ref[:, cols], preferred_element_type=F32))
        part = jnp.dot(merged.astype(BF16), wo_ref[cols, :], preferred_element_type=F32)
        if jt == 0:
            acc_ref[...] = part
        else:
            acc_ref[...] += part

    scale_ref[...] = _rms_scale(acc_ref[...])
    o_ref[...] = h_ref[...] + acc_ref[...] * scale_ref[...] * post_ref[...]


def _merge(h, pm, at, rt, gates, w_pool_out, w_attn_out, w_ret_out, w_out, post, layer):
    def resident(rows, cols):
        return pl.BlockSpec((None, rows, cols), lambda i: (layer, 0, 0), pipeline_mode=pl.Buffered(1))

    return pl.pallas_call(
        _merge_kernel,
        grid=(TOKENS // MERGE_TM,),
        in_specs=[
            pl.BlockSpec((MERGE_TM, D_MODEL), lambda i: (i, 0)),
            pl.BlockSpec((MERGE_TM, POOL_WIDTH), lambda i: (i, 0)),
            pl.BlockSpec((MERGE_TM, Q_WIDTH), lambda i: (i, 0)),
            pl.BlockSpec((MERGE_TM, RV_WIDTH), lambda i: (i, 0)),
            pl.BlockSpec((MERGE_TM, GATE_WIDTH), lambda i: (i, 0)),
            resident(POOL_WIDTH, D_MODEL),
            resident(Q_WIDTH, D_MODEL),
            resident(RV_WIDTH, D_MODEL),
            resident(D_MODEL, D_MODEL),
            pl.BlockSpec((None, 1, D_MODEL), lambda i: (layer, 0, 0)),
        ],
        out_specs=pl.BlockSpec((MERGE_TM, D_MODEL), lambda i: (i, 0)),
        out_shape=jax.ShapeDtypeStruct((TOKENS, D_MODEL), F32),
        scratch_shapes=[pltpu.VMEM((MERGE_TM, D_MODEL), F32), pltpu.VMEM((MERGE_TM, 1), F32)],
        compiler_params=_params("parallel"),
        name="merge",
    )(h, pm, at, rt, gates, w_pool_out, w_attn_out, w_ret_out, w_out, post)


def kernel(x, ffn1_pre, ffn1_up, ffn1_down, ffn1_post, mix_pre, w_in, pool_w, pool_scale, attn_sinks, ret_norm, w_pool_out, w_attn_out, w_ret_out, w_gate, w_out, mix_post, ffn2_pre, ffn2_up, ffn2_down, ffn2_post):
    def gain(a):
        return a.reshape(DEPTH, 1, a.shape[-1])

    def bf(a):
        return a.astype(BF16)

    pool_w, w_pool_out, w_attn_out, w_ret_out, w_out = (
        bf(pool_w), bf(w_pool_out), bf(w_attn_out), bf(w_ret_out), bf(w_out))
    log_g = jnp.log(1.0 - jnp.exp2(-5.0 - jnp.arange(RET_HEADS, dtype=F32)))
    sinks = attn_sinks.reshape(DEPTH, ATTN_HEADS, 1, 1)

    h = x.reshape(TOKENS, D_MODEL)
    for l in range(DEPTH):
        h = _ffn(h, gain(ffn1_pre), ffn1_up, ffn1_down, gain(ffn1_post), l)
        p = _proj(h, gain(mix_pre), w_in, l)
        gates, rt, at, pm = _mix(h, gain(mix_pre), w_gate, p, log_g, gain(ret_norm), sinks,
                                 pool_w, gain(pool_scale), l)
        h = _merge(h, pm, at, rt, gates, w_pool_out, w_attn_out, w_ret_out, w_out, gain(mix_post), l)
        h = _ffn(h, gain(ffn2_pre), ffn2_up, ffn2_down, gain(ffn2_post), l)
    return h.reshape(BATCH, SEQ, D_MODEL)
```

```python
import functools

import jax
import jax.numpy as jnp
from jax import lax
from jax.experimental import pallas as pl
from jax.experimental.pallas import tpu as pltpu

F32 = jnp.float32
BF16 = jnp.bfloat16

D_MODEL = 2048
BATCH = 4
SEQ = 2048
DEPTH = 2
TOKENS = BATCH * SEQ

POOL_WINDOWS = (2, 4, 8, 16)
POOL_GROUP = 256
POOL_WIDTH = 1024
POOL_HALO = 16
ATTN_HEADS = 16
ATTN_KV_HEADS = 4
ATTN_GROUP = ATTN_HEADS // ATTN_KV_HEADS
ATTN_HEAD_DIM = 64
ATTN_BLOCK = 128
Q_WIDTH = 1024
KV_WIDTH = 256
RET_HEADS = 4
RET_KEY_DIM = 256
RET_VALUE_DIM = 512
RQK_WIDTH = 1024
RV_WIDTH = 2048
FFN_HIDDEN = 5632
RMS_EPS = 1e-6
GATE_WIDTH = 3 * D_MODEL

OFF_XP = 0
OFF_Q = OFF_XP + POOL_WIDTH
OFF_RQ = OFF_Q + Q_WIDTH
OFF_RK = OFF_RQ + RQK_WIDTH
OFF_RV = OFF_RK + RQK_WIDTH
OFF_RG = OFF_RV + RV_WIDTH
OFF_K = OFF_RG + RV_WIDTH
OFF_V = OFF_K + KV_WIDTH
IN_WIDTH = OFF_V + KV_WIDTH
W_IN_KV_OFFSET = POOL_WIDTH + Q_WIDTH

VMEM_LIMIT_BYTES = 60 * 1024 * 1024

FFN_TM = 1024
FFN_TF = 256
FFN_TAIL_TF = 512
ROUND_BLOCKS = 64
PROJ_TM = 2048
PROJ_TN = 512
MIX_TM = 1024
MIX_ROWS = ATTN_BLOCK
MIX_STEPS = MIX_TM // MIX_ROWS
GATE_TN = GATE_WIDTH // MIX_STEPS
MERGE_TM = 256
MERGE_TJ = 1024


def _params(*sem):
    return pltpu.CompilerParams(dimension_semantics=sem, vmem_limit_bytes=VMEM_LIMIT_BYTES)


def _rms_scale(x):
    return lax.rsqrt(jnp.mean(x * x, axis=-1, keepdims=True) + RMS_EPS)


def _ffn_steps(h_ref, pre_ref, wg_ref, wu_ref, wd_ref, post_ref, o_ref, xn_ref, scale_ref, fused_gate_up):
    j = pl.program_id(1)

    def hidden_tile_step(first):
        if fused_gate_up:
            tf = wg_ref.shape[-1]
            gu = jnp.dot(xn_ref[...], jnp.concatenate([wg_ref[...], wu_ref[...]], axis=1),
                         preferred_element_type=F32)
            g, u = gu[:, :tf], gu[:, tf:]
        else:
            g = jnp.dot(xn_ref[...], wg_ref[...], preferred_element_type=F32)
            u = jnp.dot(xn_ref[...], wu_ref[...], preferred_element_type=F32)
        a = (g * jax.nn.sigmoid(g) * u).astype(BF16)
        part = jnp.dot(a, wd_ref[...], preferred_element_type=F32)
        if first:
            o_ref[...] = part
        else:
            o_ref[...] += part

    @pl.when(j == 0)
    def _():
        x = h_ref[...]
        xn_ref[...] = (x * _rms_scale(x) * pre_ref[...]).astype(BF16)
        hidden_tile_step(first=True)

    @pl.when(j > 0)
    def _():
        hidden_tile_step(first=False)

    @pl.when(j == pl.num_programs(1) - 1)
    def _():
        scale_ref[...] = _rms_scale(o_ref[...])
        o_ref[...] = h_ref[...] + 0.5 * (o_ref[...] * scale_ref[...] * post_ref[...])


def _ffn_head_kernel(h_ref, pre_ref, wg_ref, wu_ref, wd_ref, post_ref,
                     o_ref, wg_bf_ref, wu_bf_ref, wd_bf_ref, xn_ref, scale_ref):
    wg_bf_ref[...] = wg_ref[...].astype(BF16)
    wu_bf_ref[...] = wu_ref[...].astype(BF16)
    wd_bf_ref[...] = wd_ref[...].astype(BF16)
    _ffn_steps(h_ref, pre_ref, wg_bf_ref, wu_bf_ref, wd_bf_ref, post_ref, o_ref, xn_ref, scale_ref,
               fused_gate_up=True)


def _ffn_tail_kernel(*refs, n_round):
    head_out_ref, h_ref, pre_ref, wg_ref, wu_ref, wd_ref, post_ref = refs[:7]
    round_in = refs[7:7 + n_round]
    o_ref = refs[7 + n_round]
    round_out = refs[8 + n_round:8 + 2 * n_round]
    xn_ref, scale_ref = refs[8 + 2 * n_round:]
    del head_out_ref
    for src, dst in zip(round_in, round_out):
        dst[...] = src[...].astype(BF16)
    _ffn_steps(h_ref, pre_ref, wg_ref, wu_ref, wd_ref, post_ref, o_ref, xn_ref, scale_ref,
               fused_gate_up=False)


def _ffn(h, pre, w_up, w_down, post, layer, round_also=()):
    nf = FFN_HIDDEN // FFN_TF
    gain_spec = pl.BlockSpec((None, 1, D_MODEL), lambda i, j: (layer, 0, 0))
    scratch = [pltpu.VMEM((FFN_TM, D_MODEL), BF16), pltpu.VMEM((FFN_TM, 1), F32)]
    out, wg_bf, wu_bf, wd_bf = pl.pallas_call(
        _ffn_head_kernel,
        grid=(1, nf),
        in_specs=[
            pl.BlockSpec((FFN_TM, D_MODEL), lambda i, j: (0, 0)),
            gain_spec,
            pl.BlockSpec((None, D_MODEL, FFN_TF), lambda i, j: (layer, 0, j)),
            pl.BlockSpec((None, D_MODEL, FFN_TF), lambda i, j: (layer, 0, nf + j)),
            pl.BlockSpec((None, FFN_TF, D_MODEL), lambda i, j: (layer, j, 0)),
            gain_spec,
        ],
        out_specs=[
            pl.BlockSpec((FFN_TM, D_MODEL), lambda i, j: (0, 0)),
            pl.BlockSpec((D_MODEL, FFN_TF), lambda i, j: (0, j)),
            pl.BlockSpec((D_MODEL, FFN_TF), lambda i, j: (0, j)),
            pl.BlockSpec((FFN_TF, D_MODEL), lambda i, j: (j, 0)),
        ],
        out_shape=[
            jax.ShapeDtypeStruct((TOKENS, D_MODEL), F32),
            jax.ShapeDtypeStruct((D_MODEL, FFN_HIDDEN), BF16),
            jax.ShapeDtypeStruct((D_MODEL, FFN_HIDDEN), BF16),
            jax.ShapeDtypeStruct((FFN_HIDDEN, D_MODEL), BF16),
        ],
        scratch_shapes=scratch,
        compiler_params=_params("arbitrary", "arbitrary"),
        name="ffn_head",
    )(h, pre, w_up, w_up, w_down, post)

    nf2 = FFN_HIDDEN // FFN_TAIL_TF
    n_tiles = TOKENS // FFN_TM - 1
    assert ROUND_BLOCKS <= n_tiles * nf2

    def round_block(i, j):
        return jnp.minimum(i * nf2 + j, ROUND_BLOCKS - 1)

    round_in_specs, round_out_specs, round_shapes = [], [], []
    for w in round_also:
        rows, cols = w.shape[1:]
        block_rows = rows // ROUND_BLOCKS
        round_in_specs.append(pl.BlockSpec((None, block_rows, cols), lambda i, j: (layer, round_block(i, j), 0)))
        round_out_specs.append(pl.BlockSpec((block_rows, cols), lambda i, j: (round_block(i, j), 0)))
        round_shapes.append(jax.ShapeDtypeStruct((rows, cols), BF16))

    out, *rounded = pl.pallas_call(
        functools.partial(_ffn_tail_kernel, n_round=len(round_also)),
        grid=(n_tiles, nf2),
        in_specs=[
            pl.BlockSpec(memory_space=pl.ANY),
            pl.BlockSpec((FFN_TM, D_MODEL), lambda i, j: (i + 1, 0)),
            gain_spec,
            pl.BlockSpec((D_MODEL, FFN_TAIL_TF), lambda i, j: (0, j)),
            pl.BlockSpec((D_MODEL, FFN_TAIL_TF), lambda i, j: (0, j)),
            pl.BlockSpec((FFN_TAIL_TF, D_MODEL), lambda i, j: (j, 0)),
            gain_spec,
        ] + round_in_specs,
        out_specs=[pl.BlockSpec((FFN_TM, D_MODEL), lambda i, j: (i + 1, 0))] + round_out_specs,
        out_shape=[jax.ShapeDtypeStruct((TOKENS, D_MODEL), F32)] + round_shapes,
        input_output_aliases={0: 0},
        scratch_shapes=scratch,
        compiler_params=_params("arbitrary", "arbitrary"),
        name="ffn_tail",
    )(out, h, pre, wg_bf, wu_bf, wd_bf, post, *round_also)
    return out, rounded


def _proj_kernel(h_ref, pre_ref, w_ref, o_ref, xn_ref):
    @pl.when(pl.program_id(1) == 0)
    def _():
        x = h_ref[...]
        xn_ref[...] = (x * _rms_scale(x) * pre_ref[...]).astype(BF16)

    o_ref[...] = jnp.dot(xn_ref[...], w_ref[...].astype(BF16),
                         preferred_element_type=F32).astype(o_ref.dtype)


def _proj(h, pre, w_in, layer):
    kv_tile = W_IN_KV_OFFSET // PROJ_TN
    last_tile = IN_WIDTH // PROJ_TN - 1

    def w_tile(j):
        return jnp.where(j < kv_tile, j, jnp.where(j < last_tile, j + 1, kv_tile))

    return pl.pallas_call(
        _proj_kernel,
        grid=(TOKENS // PROJ_TM, IN_WIDTH // PROJ_TN),
        in_specs=[
            pl.BlockSpec((PROJ_TM, D_MODEL), lambda i, j: (i, 0)),
            pl.BlockSpec((None, 1, D_MODEL), lambda i, j: (layer, 0, 0)),
            pl.BlockSpec((None, D_MODEL, PROJ_TN), lambda i, j: (layer, 0, w_tile(j))),
        ],
        out_specs=pl.BlockSpec((PROJ_TM, PROJ_TN), lambda i, j: (i, j)),
        out_shape=jax.ShapeDtypeStruct((TOKENS, IN_WIDTH), BF16),
        scratch_shapes=[pltpu.VMEM((PROJ_TM, D_MODEL), BF16)],
        compiler_params=_params("parallel", "arbitrary"),
        name="proj",
    )(h, pre, w_in)


def _pool_step(xc_ref, xp_ref, pw_ref, scale_ref, o_ref, seq_block):
    cur = xc_ref[...].astype(F32)
    prev = jnp.where(seq_block == 0, 0.0, xp_ref[...].astype(F32))
    ext = jnp.concatenate([prev, cur], axis=0)
    t = seq_block * MIX_ROWS + lax.broadcasted_iota(jnp.int32, (MIX_ROWS, 1), 0)
    for gi, w in enumerate(POOL_WINDOWS):
        cols = slice(gi * POOL_GROUP, (gi + 1) * POOL_GROUP)
        s = ext[:, cols]
        span = 1
        while span < w:
            s = s + pltpu.roll(s, span, 0)
            span *= 2
        wsum = s[POOL_HALO:, :]
        count = jnp.minimum(t + 1, w).astype(F32)
        pooled = wsum / count - cur[:, cols]
        mixed = jnp.dot(pooled.astype(BF16), pw_ref[gi], preferred_element_type=F32)
        o_ref[:, cols] = (mixed * scale_ref[:, cols]).astype(o_ref.dtype)


def _attn_fold_mask():
    ci = lax.broadcasted_iota(jnp.int32, (ATTN_BLOCK, ATTN_BLOCK), 0)
    qi = lax.broadcasted_iota(jnp.int32, (ATTN_BLOCK, ATTN_BLOCK), 1)
    return ci, qi, ci > qi


def _attn_init_bias(bias_ref):
    ci, qi, from_prev = _attn_fold_mask()
    dist = jnp.where(from_prev, qi + ATTN_BLOCK - ci, qi - ci).astype(F32)
    for h in range(ATTN_HEADS):
        alibi = -(2.0 ** (-8.0 * (h + 1) / ATTN_HEADS)) * dist
        bias_ref[0, h] = alibi
        bias_ref[1, h] = jnp.where(from_prev, -jnp.inf, alibi)


def _attn_step(sink_ref, q_ref, kc_ref, kp_ref, vc_ref, vp_ref, o_ref,
               bias_ref, s_ref, e_ref, denom_ref, seq_block):
    _, _, from_prev = _attn_fold_mask()
    seq_start = (seq_block == 0).astype(jnp.int32)
    q = q_ref[...] * (ATTN_HEAD_DIM ** -0.5)
    kk = jnp.concatenate([kp_ref[...], kc_ref[...]], axis=0)
    vv = jnp.concatenate([vp_ref[...], vc_ref[...]], axis=0)
    vv_t = vv.astype(F32).T.astype(BF16)

    for h in range(ATTN_HEADS):
        kv_cols = slice((h // ATTN_GROUP) * ATTN_HEAD_DIM, (h // ATTN_GROUP + 1) * ATTN_HEAD_DIM)
        s2 = lax.dot_general(kk[:, kv_cols], q[:, h * ATTN_HEAD_DIM:(h + 1) * ATTN_HEAD_DIM],
                             (((1,), (1,)), ((), ())), preferred_element_type=F32)
        s_ref[h] = jnp.where(from_prev, s2[:ATTN_BLOCK], s2[ATTN_BLOCK:]) + bias_ref[seq_start, h]

    s = s_ref[...]
    sink = sink_ref[...]
    m = jnp.maximum(jnp.max(s, axis=1, keepdims=True), sink)
    e = jnp.exp(s - m)
    denom_ref[...] = jnp.sum(e, axis=1, keepdims=True) + jnp.exp(sink - m)
    eb = e.astype(BF16)
    zero = jnp.zeros_like(eb)
    e_ref[:, :ATTN_BLOCK, :] = jnp.where(from_prev, eb, zero)
    e_ref[:, ATTN_BLOCK:, :] = jnp.where(from_prev, zero, eb)

    for pair in range(ATTN_HEADS // 2):
        outs = []
        for h in (2 * pair, 2 * pair + 1):
            kv_rows = slice((h // ATTN_GROUP) * ATTN_HEAD_DIM, (h // ATTN_GROUP + 1) * ATTN_HEAD_DIM)
            pv_t = jnp.dot(vv_t[kv_rows], e_ref[h], preferred_element_type=F32)
            outs.append(pv_t / denom_ref[h])
        pair_cols = slice(2 * pair * ATTN_HEAD_DIM, (2 * pair + 2) * ATTN_HEAD_DIM)
        o_ref[:, pair_cols] = jnp.concatenate(outs, axis=0).T.astype(o_ref.dtype)


def _ret_init_tables(logg_ref, intra_ref, qdec_ref, kdec_ref):
    key_scale = RET_KEY_DIM ** -0.5
    pos = lax.broadcasted_iota(jnp.int32, (MIX_ROWS, 1), 0).astype(F32)
    row = lax.broadcasted_iota(jnp.int32, (MIX_ROWS, MIX_ROWS), 0)
    col = lax.broadcasted_iota(jnp.int32, (MIX_ROWS, MIX_ROWS), 1)
    diff = (row - col).astype(F32)
    for h in range(RET_HEADS):
        lg = logg_ref[h]
        intra_ref[h] = jnp.where(diff >= 0, jnp.exp(lg * jnp.maximum(diff, 0.0)), 0.0) * key_scale
        qdec_ref[h] = jnp.exp(lg * (pos + 1.0))
        kdec_ref[h] = jnp.exp(lg * (MIX_ROWS - 1.0 - pos)) * key_scale


def _ret_step(logg_ref, q_ref, k_ref, v_ref, g_ref, norm_ref, o_ref,
              state_ref, intra_ref, qdec_ref, kdec_ref):
    for h in range(RET_HEADS):
        k_cols = slice(h * RET_KEY_DIM, (h + 1) * RET_KEY_DIM)
        v_cols = slice(h * RET_VALUE_DIM, (h + 1) * RET_VALUE_DIM)
        chunk_decay = jnp.exp(jnp.full((1, 1), MIX_ROWS, F32) * logg_ref[h])
        q = q_ref[:, k_cols]
        k = k_ref[:, k_cols]
        v = v_ref[:, v_cols]
        state = state_ref[h]
        inner = lax.dot_general(q, k, (((1,), (1,)), ((), ())), preferred_element_type=F32) * intra_ref[h]
        qd = (q.astype(F32) * qdec_ref[h]).astype(BF16)
        o = (jnp.dot(inner.astype(BF16), v, preferred_element_type=F32)
             + jnp.dot(qd, state.astype(BF16), preferred_element_type=F32))
        kd = (k.astype(F32) * kdec_ref[h]).astype(BF16)
        state_ref[h] = state * chunk_decay + lax.dot_general(
            kd, v, (((0,), (0,)), ((), ())), preferred_element_type=F32)

        o = o * _rms_scale(o) * norm_ref[:, v_cols]
        g = g_ref[:, v_cols].astype(F32)
        o_ref[:, v_cols] = (g * jax.nn.sigmoid(g) * o).astype(o_ref.dtype)


def _mix_kernel(logg_ref, h_ref, pre_ref, wgate_ref,
                rq_ref, rk_ref, rv_ref, rg_ref, norm_ref,
                sink_ref, aq_ref, kc_ref, kp_ref, vc_ref, vp_ref,
                xc_ref, xp_ref, pw_ref, pscale_ref,
                gates_ref, rt_ref, at_ref, pm_ref,
                xn_ref, state_ref, intra_ref, qdec_ref, kdec_ref,
                bias_ref, s_ref, e_ref, denom_ref):
    i = pl.program_id(0)
    j = pl.program_id(1)
    seq_block = (i * MIX_STEPS + j) % (SEQ // MIX_ROWS)

    @pl.when((i == 0) & (j == 0))
    def _():
        _ret_init_tables(logg_ref, intra_ref, qdec_ref, kdec_ref)
        _attn_init_bias(bias_ref)

    @pl.when(seq_block == 0)
    def _():
        state_ref[...] = jnp.zeros_like(state_ref)

    def step():
        gates_ref[...] = jnp.dot(xn_ref[...], wgate_ref[...],
                                 preferred_element_type=F32).astype(gates_ref.dtype)
        _ret_step(logg_ref, rq_ref, rk_ref, rv_ref, rg_ref, norm_ref, rt_ref,
                  state_ref, intra_ref, qdec_ref, kdec_ref)
        _attn_step(sink_ref, aq_ref, kc_ref, kp_ref, vc_ref, vp_ref, at_ref,
                   bias_ref, s_ref, e_ref, denom_ref, seq_block)
        _pool_step(xc_ref, xp_ref, pw_ref, pscale_ref, pm_ref, seq_block)

    @pl.when(j == 0)
    def _():
        x = h_ref[...]
        xn_ref[...] = (x * _rms_scale(x) * pre_ref[...]).astype(BF16)
        step()

    pl.when(j > 0)(step)


def _mix(h, pre, w_gate, p, log_g, ret_norm, sinks, pool_w, pool_scale, layer):
    def row(i, j):
        return i * MIX_STEPS + j

    def prev_row(i, j):
        return jnp.maximum(row(i, j) - 1, 0)

    def slab(width, offset):
        return pl.BlockSpec((MIX_ROWS, width), lambda i, j: (row(i, j), offset // width))

    halo_blocks = MIX_ROWS // POOL_HALO
    kcol = OFF_K // KV_WIDTH
    vcol = OFF_V // KV_WIDTH
    in_specs = [
        pl.BlockSpec(memory_space=pltpu.SMEM),
        pl.BlockSpec((MIX_TM, D_MODEL), lambda i, j: (i, 0)),
        pl.BlockSpec((None, 1, D_MODEL), lambda i, j: (layer, 0, 0)),
        pl.BlockSpec((D_MODEL, GATE_TN), lambda i, j: (0, j)),
        slab(RQK_WIDTH, OFF_RQ), slab(RQK_WIDTH, OFF_RK), slab(RV_WIDTH, OFF_RV), slab(RV_WIDTH, OFF_RG),
        pl.BlockSpec((None, 1, RV_WIDTH), lambda i, j: (layer, 0, 0)),
        pl.BlockSpec((None, ATTN_HEADS, 1, 1), lambda i, j: (layer, 0, 0, 0)),
        slab(Q_WIDTH, OFF_Q),
        pl.BlockSpec((MIX_ROWS, KV_WIDTH), lambda i, j: (row(i, j), kcol)),
        pl.BlockSpec((MIX_ROWS, KV_WIDTH), lambda i, j: (prev_row(i, j), kcol)),
        pl.BlockSpec((MIX_ROWS, KV_WIDTH), lambda i, j: (row(i, j), vcol)),
        pl.BlockSpec((MIX_ROWS, KV_WIDTH), lambda i, j: (prev_row(i, j), vcol)),
        slab(POOL_WIDTH, OFF_XP),
        pl.BlockSpec((POOL_HALO, POOL_WIDTH),
                     lambda i, j: (jnp.maximum(row(i, j) * halo_blocks - 1, 0), 0)),
        pl.BlockSpec((None, len(POOL_WINDOWS), POOL_GROUP, POOL_GROUP), lambda i, j: (layer, 0, 0, 0)),
        pl.BlockSpec((None, 1, POOL_WIDTH), lambda i, j: (layer, 0, 0)),
    ]
    return pl.pallas_call(
        _mix_kernel,
        grid=(TOKENS // MIX_TM, MIX_STEPS),
        in_specs=in_specs,
        out_specs=[
            pl.BlockSpec((MIX_TM, GATE_TN), lambda i, j: (i, j)),
            pl.BlockSpec((MIX_ROWS, RV_WIDTH), lambda i, j: (row(i, j), 0)),
            pl.BlockSpec((MIX_ROWS, Q_WIDTH), lambda i, j: (row(i, j), 0)),
            pl.BlockSpec((MIX_ROWS, POOL_WIDTH), lambda i, j: (row(i, j), 0)),
        ],
        out_shape=[
            jax.ShapeDtypeStruct((TOKENS, GATE_WIDTH), BF16),
            jax.ShapeDtypeStruct((TOKENS, RV_WIDTH), BF16),
            jax.ShapeDtypeStruct((TOKENS, Q_WIDTH), BF16),
            jax.ShapeDtypeStruct((TOKENS, POOL_WIDTH), BF16),
        ],
        scratch_shapes=[
            pltpu.VMEM((MIX_TM, D_MODEL), BF16),
            pltpu.VMEM((RET_HEADS, RET_KEY_DIM, RET_VALUE_DIM), F32),
            pltpu.VMEM((RET_HEADS, MIX_ROWS, MIX_ROWS), F32),
            pltpu.VMEM((RET_HEADS, MIX_ROWS, 1), F32),
            pltpu.VMEM((RET_HEADS, MIX_ROWS, 1), F32),
            pltpu.VMEM((2, ATTN_HEADS, ATTN_BLOCK, ATTN_BLOCK), F32),
            pltpu.VMEM((ATTN_HEADS, ATTN_BLOCK, ATTN_BLOCK), F32),
            pltpu.VMEM((ATTN_HEADS, 2 * ATTN_BLOCK, ATTN_BLOCK), BF16),
            pltpu.VMEM((ATTN_HEADS, 1, ATTN_BLOCK), F32),
        ],
        compiler_params=_params("arbitrary", "arbitrary"),
        name="mix",
    )(log_g, h, pre, w_gate, p, p, p, p, ret_norm, sinks,
      p, p, p, p, p, p, p, pool_w, pool_scale)


def _merge_kernel(h_ref, pm_ref, at_ref, rt_ref, gates_ref,
                  wp_ref, wa_ref, wr_ref, wo_ref, post_ref, o_ref, acc_ref, scale_ref):
    def gate(branch, cols):
        return jax.nn.sigmoid(gates_ref[:, branch * D_MODEL + cols.start:branch * D_MODEL + cols.stop]
                              .astype(F32))

    for jt in range(D_MODEL // MERGE_TJ):
        cols = slice(jt * MERGE_TJ, (jt + 1) * MERGE_TJ)
        merged = (gate(0, cols) * jnp.dot(pm_ref[...], wp_ref[:, cols], preferred_element_type=F32)
                  + gate(1, cols) * jnp.dot(at_ref[...], wa_ref[:, cols], preferred_element_type=F32)
                  + gate(2, cols) * jnp.dot(rt_ref[...], wr_ref[:, cols], preferred_element_type=F32))
        part = jnp.dot(merged.astype(BF16), wo_ref[cols, :], preferred_element_type=F32)
        if jt == 0:
            acc_ref[...] = part
        else:
            acc_ref[...] += part

    scale_ref[...] = _rms_scale(acc_ref[...])
    o_ref[...] = h_ref[...] + acc_ref[...] * scale_ref[...] * post_ref[...]


def _merge(h, pm, at, rt, gates, w_pool_out, w_attn_out, w_ret_out, w_out, post, layer):
    def resident(rows, cols):
        return pl.BlockSpec((rows, cols), lambda i: (0, 0), pipeline_mode=pl.Buffered(1))

    return pl.pallas_call(
        _merge_kernel,
        grid=(TOKENS // MERGE_TM,),
        in_specs=[
            pl.BlockSpec((MERGE_TM, D_MODEL), lambda i: (i, 0)),
            pl.BlockSpec((MERGE_TM, POOL_WIDTH), lambda i: (i, 0)),
            pl.BlockSpec((MERGE_TM, Q_WIDTH), lambda i: (i, 0)),
            pl.BlockSpec((MERGE_TM, RV_WIDTH), lambda i: (i, 0)),
            pl.BlockSpec((MERGE_TM, GATE_WIDTH), lambda i: (i, 0)),
            resident(POOL_WIDTH, D_MODEL),
            resident(Q_WIDTH, D_MODEL),
            resident(RV_WIDTH, D_MODEL),
            resident(D_MODEL, D_MODEL),
            pl.BlockSpec((None, 1, D_MODEL), lambda i: (layer, 0, 0)),
        ],
        out_specs=pl.BlockSpec((MERGE_TM, D_MODEL), lambda i: (i, 0)),
        out_shape=jax.ShapeDtypeStruct((TOKENS, D_MODEL), F32),
        scratch_shapes=[pltpu.VMEM((MERGE_TM, D_MODEL), F32), pltpu.VMEM((MERGE_TM, 1), F32)],
        compiler_params=_params("parallel"),
        name="merge",
    )(h, pm, at, rt, gates, w_pool_out, w_attn_out, w_ret_out, w_out, post)


def kernel(x, ffn1_pre, ffn1_up, ffn1_down, ffn1_post, mix_pre, w_in, pool_w, pool_scale, attn_sinks, ret_norm, w_pool_out, w_attn_out, w_ret_out, w_gate, w_out, mix_post, ffn2_pre, ffn2_up, ffn2_down, ffn2_post):
    def gain(a):
        return a.reshape(DEPTH, 1, a.shape[-1])

    pool_w = pool_w.astype(BF16)
    log_g = jnp.log(1.0 - jnp.exp2(-5.0 - jnp.arange(RET_HEADS, dtype=F32)))
    sinks = attn_sinks.reshape(DEPTH, ATTN_HEADS, 1, 1)

    h = x.reshape(TOKENS, D_MODEL)
    for l in range(DEPTH):
        h, (w_gate_bf, w_pool_bf, w_attn_bf, w_ret_bf, w_out_bf) = _ffn(
            h, gain(ffn1_pre), ffn1_up, ffn1_down, gain(ffn1_post), l,
            round_also=(w_gate, w_pool_out, w_attn_out, w_ret_out, w_out))
        p = _proj(h, gain(mix_pre), w_in, l)
        gates, rt, at, pm = _mix(h, gain(mix_pre), w_gate_bf, p, log_g, gain(ret_norm), sinks,
                                 pool_w, gain(pool_scale), l)
        h = _merge(h, pm, at, rt, gates, w_pool_bf, w_attn_bf, w_ret_bf, w_out_bf, gain(mix_post), l)
        h, _ = _ffn(h, gain(ffn2_pre), ffn2_up, ffn2_down, gain(ffn2_post), l)
    return h.reshape(BATCH, SEQ, D_MODEL)
```

```python
import functools

import jax
import jax.numpy as jnp
from jax import lax
from jax.experimental import pallas as pl
from jax.experimental.pallas import tpu as pltpu

F32 = jnp.float32
BF16 = jnp.bfloat16

D_MODEL = 2048
BATCH = 4
SEQ = 2048
DEPTH = 2
TOKENS = BATCH * SEQ

POOL_WINDOWS = (2, 4, 8, 16)
POOL_GROUP = 256
POOL_WIDTH = 1024
POOL_HALO = 16
ATTN_HEADS = 16
ATTN_KV_HEADS = 4
ATTN_GROUP = ATTN_HEADS // ATTN_KV_HEADS
ATTN_HEAD_DIM = 64
ATTN_BLOCK = 128
Q_WIDTH = 1024
KV_WIDTH = 256
RET_HEADS = 4
RET_KEY_DIM = 256
RET_VALUE_DIM = 512
RQK_WIDTH = 1024
RV_WIDTH = 2048
FFN_HIDDEN = 5632
RMS_EPS = 1e-6
GATE_WIDTH = 3 * D_MODEL

OFF_XP = 0
OFF_Q = OFF_XP + POOL_WIDTH
OFF_RQ = OFF_Q + Q_WIDTH
OFF_RK = OFF_RQ + RQK_WIDTH
OFF_RV = OFF_RK + RQK_WIDTH
OFF_RG = OFF_RV + RV_WIDTH
OFF_K = OFF_RG + RV_WIDTH
OFF_V = OFF_K + KV_WIDTH
IN_WIDTH = OFF_V + KV_WIDTH
W_IN_KV_OFFSET = POOL_WIDTH + Q_WIDTH

VMEM_LIMIT_BYTES = 60 * 1024 * 1024

FFN_TM = 1024
FFN_TF = 256
FFN_TAIL_TF = 512
ROUND_BLOCKS = 64
PROJ_TM = 2048
PROJ_TN = 512
MIX_TM = 1024
MIX_ROWS = ATTN_BLOCK
MIX_STEPS = MIX_TM // MIX_ROWS
GATE_TN = GATE_WIDTH // MIX_STEPS
MERGE_TM = 256
MERGE_TJ = 1024


def _params(*sem):
    return pltpu.CompilerParams(dimension_semantics=sem, vmem_limit_bytes=VMEM_LIMIT_BYTES)


def _rms_scale(x):
    return lax.rsqrt(jnp.mean(x * x, axis=-1, keepdims=True) + RMS_EPS)


def _ffn_steps(h_ref, pre_ref, wg_ref, wu_ref, wd_ref, post_ref, o_ref, xn_ref, scale_ref, fused_gate_up):
    j = pl.program_id(1)

    def hidden_tile_step(first):
        if fused_gate_up:
            tf = wg_ref.shape[-1]
            gu = jnp.dot(xn_ref[...], jnp.concatenate([wg_ref[...], wu_ref[...]], axis=1),
                         preferred_element_type=F32)
            g, u = gu[:, :tf], gu[:, tf:]
        else:
            g = jnp.dot(xn_ref[...], wg_ref[...], preferred_element_type=F32)
            u = jnp.dot(xn_ref[...], wu_ref[...], preferred_element_type=F32)
        a = (g * jax.nn.sigmoid(g) * u).astype(BF16)
        part = jnp.dot(a, wd_ref[...], preferred_element_type=F32)
        if first:
            o_ref[...] = part
        else:
            o_ref[...] += part

    @pl.when(j == 0)
    def _():
        x = h_ref[...]
        xn_ref[...] = (x * _rms_scale(x) * pre_ref[...]).astype(BF16)
        hidden_tile_step(first=True)

    @pl.when(j > 0)
    def _():
        hidden_tile_step(first=False)

    @pl.when(j == pl.num_programs(1) - 1)
    def _():
        scale_ref[...] = _rms_scale(o_ref[...])
        o_ref[...] = h_ref[...] + 0.5 * (o_ref[...] * scale_ref[...] * post_ref[...])


def _ffn_head_kernel(*refs, down_is_rounded):
    h_ref, pre_ref, wg_ref, wu_ref, wd_ref, post_ref, o_ref, wg_bf_ref, wu_bf_ref = refs[:9]
    xn_ref, scale_ref = refs[-2:]
    wg_bf_ref[...] = wg_ref[...].astype(BF16)
    wu_bf_ref[...] = wu_ref[...].astype(BF16)
    if down_is_rounded:
        wd_bf_ref = wd_ref
    else:
        wd_bf_ref = refs[9]
        wd_bf_ref[...] = wd_ref[...].astype(BF16)
    _ffn_steps(h_ref, pre_ref, wg_bf_ref, wu_bf_ref, wd_bf_ref, post_ref, o_ref, xn_ref, scale_ref,
               fused_gate_up=True)


def _ffn_tail_kernel(*refs, n_round):
    head_out_ref, h_ref, pre_ref, wg_ref, wu_ref, wd_ref, post_ref = refs[:7]
    round_in = refs[7:7 + n_round]
    o_ref = refs[7 + n_round]
    round_out = refs[8 + n_round:8 + 2 * n_round]
    xn_ref, scale_ref = refs[8 + 2 * n_round:]
    del head_out_ref
    for src, dst in zip(round_in, round_out):
        dst[...] = src[...].astype(BF16)
    _ffn_steps(h_ref, pre_ref, wg_ref, wu_ref, wd_ref, post_ref, o_ref, xn_ref, scale_ref,
               fused_gate_up=False)


def _round_specs(arrays, layer, step_index):
    def block(i, j):
        return jnp.minimum(step_index(i, j), ROUND_BLOCKS - 1)

    in_specs, out_specs, shapes = [], [], []
    for w in arrays:
        rows, cols = w.shape[1:]
        block_rows = rows // ROUND_BLOCKS
        in_specs.append(pl.BlockSpec((None, block_rows, cols), lambda i, j: (layer, block(i, j), 0)))
        out_specs.append(pl.BlockSpec((block_rows, cols), lambda i, j: (block(i, j), 0)))
        shapes.append(jax.ShapeDtypeStruct((rows, cols), BF16))
    return in_specs, out_specs, shapes


def _ffn(h, pre, w_up, w_down, post, layer, round_also=(), round_layer=None, down_bf=None):
    nf = FFN_HIDDEN // FFN_TF
    gain_spec = pl.BlockSpec((None, 1, D_MODEL), lambda i, j: (layer, 0, 0))
    scratch = [pltpu.VMEM((FFN_TM, D_MODEL), BF16), pltpu.VMEM((FFN_TM, 1), F32)]
    if down_bf is None:
        down_operand = w_down
        down_spec = pl.BlockSpec((None, FFN_TF, D_MODEL), lambda i, j: (layer, j, 0))
        down_out_specs = [pl.BlockSpec((FFN_TF, D_MODEL), lambda i, j: (j, 0))]
        down_out_shapes = [jax.ShapeDtypeStruct((FFN_HIDDEN, D_MODEL), BF16)]
    else:
        down_operand = down_bf
        down_spec = pl.BlockSpec((FFN_TF, D_MODEL), lambda i, j: (j, 0))
        down_out_specs, down_out_shapes = [], []
    out, wg_bf, wu_bf, *maybe_down = pl.pallas_call(
        functools.partial(_ffn_head_kernel, down_is_rounded=down_bf is not None),
        grid=(1, nf),
        in_specs=[
            pl.BlockSpec((FFN_TM, D_MODEL), lambda i, j: (0, 0)),
            gain_spec,
            pl.BlockSpec((None, D_MODEL, FFN_TF), lambda i, j: (layer, 0, j)),
            pl.BlockSpec((None, D_MODEL, FFN_TF), lambda i, j: (layer, 0, nf + j)),
            down_spec,
            gain_spec,
        ],
        out_specs=[
            pl.BlockSpec((FFN_TM, D_MODEL), lambda i, j: (0, 0)),
            pl.BlockSpec((D_MODEL, FFN_TF), lambda i, j: (0, j)),
            pl.BlockSpec((D_MODEL, FFN_TF), lambda i, j: (0, j)),
        ] + down_out_specs,
        out_shape=[
            jax.ShapeDtypeStruct((TOKENS, D_MODEL), F32),
            jax.ShapeDtypeStruct((D_MODEL, FFN_HIDDEN), BF16),
            jax.ShapeDtypeStruct((D_MODEL, FFN_HIDDEN), BF16),
        ] + down_out_shapes,
        scratch_shapes=scratch,
        compiler_params=_params("arbitrary", "arbitrary"),
        name="ffn_head",
    )(h, pre, w_up, w_up, down_operand, post)
    wd_bf = down_bf if down_bf is not None else maybe_down[0]

    nf2 = FFN_HIDDEN // FFN_TAIL_TF
    n_tiles = TOKENS // FFN_TM - 1
    assert ROUND_BLOCKS <= n_tiles * nf2
    round_in_specs, round_out_specs, round_shapes = _round_specs(
        round_also, layer if round_layer is None else round_layer, lambda i, j: i * nf2 + j)

    out, *rounded = pl.pallas_call(
        functools.partial(_ffn_tail_kernel, n_round=len(round_also)),
        grid=(n_tiles, nf2),
        in_specs=[
            pl.BlockSpec(memory_space=pl.ANY),
            pl.BlockSpec((FFN_TM, D_MODEL), lambda i, j: (i + 1, 0)),
            gain_spec,
            pl.BlockSpec((D_MODEL, FFN_TAIL_TF), lambda i, j: (0, j)),
            pl.BlockSpec((D_MODEL, FFN_TAIL_TF), lambda i, j: (0, j)),
            pl.BlockSpec((FFN_TAIL_TF, D_MODEL), lambda i, j: (j, 0)),
            gain_spec,
        ] + round_in_specs,
        out_specs=[pl.BlockSpec((FFN_TM, D_MODEL), lambda i, j: (i + 1, 0))] + round_out_specs,
        out_shape=[jax.ShapeDtypeStruct((TOKENS, D_MODEL), F32)] + round_shapes,
        input_output_aliases={0: 0},
        scratch_shapes=scratch,
        compiler_params=_params("arbitrary", "arbitrary"),
        name="ffn_tail",
    )(out, h, pre, wg_bf, wu_bf, wd_bf, post, *round_also)
    return out, rounded


def _proj_kernel(h_ref, pre_ref, w_ref, o_ref, xn_ref):
    @pl.when(pl.program_id(1) == 0)
    def _():
        x = h_ref[...]
        xn_ref[...] = (x * _rms_scale(x) * pre_ref[...]).astype(BF16)

    o_ref[...] = jnp.dot(xn_ref[...], w_ref[...].astype(BF16),
                         preferred_element_type=F32).astype(o_ref.dtype)


def _proj(h, pre, w_in, layer):
    kv_tile = W_IN_KV_OFFSET // PROJ_TN
    last_tile = IN_WIDTH // PROJ_TN - 1

    def w_tile(j):
        return jnp.where(j < kv_tile, j, jnp.where(j < last_tile, j + 1, kv_tile))

    return pl.pallas_call(
        _proj_kernel,
        grid=(TOKENS // PROJ_TM, IN_WIDTH // PROJ_TN),
        in_specs=[
            pl.BlockSpec((PROJ_TM, D_MODEL), lambda i, j: (i, 0)),
            pl.BlockSpec((None, 1, D_MODEL), lambda i, j: (layer, 0, 0)),
            pl.BlockSpec((None, D_MODEL, PROJ_TN), lambda i, j: (layer, 0, w_tile(j))),
        ],
        out_specs=pl.BlockSpec((PROJ_TM, PROJ_TN), lambda i, j: (i, j)),
        out_shape=jax.ShapeDtypeStruct((TOKENS, IN_WIDTH), BF16),
        scratch_shapes=[pltpu.VMEM((PROJ_TM, D_MODEL), BF16)],
        compiler_params=_params("parallel", "arbitrary"),
        name="proj",
    )(h, pre, w_in)


def _pool_step(xc_ref, xp_ref, pw_ref, scale_ref, o_ref, seq_block):
    cur = xc_ref[...].astype(F32)
    prev = jnp.where(seq_block == 0, 0.0, xp_ref[...].astype(F32))
    ext = jnp.concatenate([prev, cur], axis=0)
    t = seq_block * MIX_ROWS + lax.broadcasted_iota(jnp.int32, (MIX_ROWS, 1), 0)
    for gi, w in enumerate(POOL_WINDOWS):
        cols = slice(gi * POOL_GROUP, (gi + 1) * POOL_GROUP)
        s = ext[:, cols]
        span = 1
        while span < w:
            s = s + pltpu.roll(s, span, 0)
            span *= 2
        wsum = s[POOL_HALO:, :]
        count = jnp.minimum(t + 1, w).astype(F32)
        pooled = wsum / count - cur[:, cols]
        mixed = jnp.dot(pooled.astype(BF16), pw_ref[gi], preferred_element_type=F32)
        o_ref[:, cols] = (mixed * scale_ref[:, cols]).astype(o_ref.dtype)


def _attn_fold_mask():
    ci = lax.broadcasted_iota(jnp.int32, (ATTN_BLOCK, ATTN_BLOCK), 0)
    qi = lax.broadcasted_iota(jnp.int32, (ATTN_BLOCK, ATTN_BLOCK), 1)
    return ci, qi, ci > qi


def _attn_init_bias(bias_ref):
    ci, qi, from_prev = _attn_fold_mask()
    dist = jnp.where(from_prev, qi + ATTN_BLOCK - ci, qi - ci).astype(F32)
    for h in range(ATTN_HEADS):
        alibi = -(2.0 ** (-8.0 * (h + 1) / ATTN_HEADS)) * dist
        bias_ref[0, h] = alibi
        bias_ref[1, h] = jnp.where(from_prev, -jnp.inf, alibi)


def _attn_step(sink_ref, q_ref, kc_ref, kp_ref, vc_ref, vp_ref, o_ref,
               bias_ref, s_ref, e_ref, denom_ref, seq_block):
    _, _, from_prev = _attn_fold_mask()
    seq_start = (seq_block == 0).astype(jnp.int32)
    q = q_ref[...] * (ATTN_HEAD_DIM ** -0.5)
    kk = jnp.concatenate([kp_ref[...], kc_ref[...]], axis=0)
    vv = jnp.concatenate([vp_ref[...], vc_ref[...]], axis=0)
    vv_t = vv.astype(F32).T.astype(BF16)

    for h in range(ATTN_HEADS):
        kv_cols = slice((h // ATTN_GROUP) * ATTN_HEAD_DIM, (h // ATTN_GROUP + 1) * ATTN_HEAD_DIM)
        s2 = lax.dot_general(kk[:, kv_cols], q[:, h * ATTN_HEAD_DIM:(h + 1) * ATTN_HEAD_DIM],
                             (((1,), (1,)), ((), ())), preferred_element_type=F32)
        s_ref[h] = jnp.where(from_prev, s2[:ATTN_BLOCK], s2[ATTN_BLOCK:]) + bias_ref[seq_start, h]

    s = s_ref[...]
    sink = sink_ref[...]
    m = jnp.maximum(jnp.max(s, axis=1, keepdims=True), sink)
    e = jnp.exp(s - m)
    denom_ref[...] = jnp.sum(e, axis=1, keepdims=True) + jnp.exp(sink - m)
    eb = e.astype(BF16)
    zero = jnp.zeros_like(eb)
    e_ref[:, :ATTN_BLOCK, :] = jnp.where(from_prev, eb, zero)
    e_ref[:, ATTN_BLOCK:, :] = jnp.where(from_prev, zero, eb)

    for pair in range(ATTN_HEADS // 2):
        outs = []
        for h in (2 * pair, 2 * pair + 1):
            kv_rows = slice((h // ATTN_GROUP) * ATTN_HEAD_DIM, (h // ATTN_GROUP + 1) * ATTN_HEAD_DIM)
            pv_t = jnp.dot(vv_t[kv_rows], e_ref[h], preferred_element_type=F32)
            outs.append(pv_t / denom_ref[h])
        pair_cols = slice(2 * pair * ATTN_HEAD_DIM, (2 * pair + 2) * ATTN_HEAD_DIM)
        o_ref[:, pair_cols] = jnp.concatenate(outs, axis=0).T.astype(o_ref.dtype)


def _ret_init_tables(logg_ref, intra_ref, qdec_ref, kdec_ref):
    key_scale = RET_KEY_DIM ** -0.5
    pos = lax.broadcasted_iota(jnp.int32, (MIX_ROWS, 1), 0).astype(F32)
    row = lax.broadcasted_iota(jnp.int32, (MIX_ROWS, MIX_ROWS), 0)
    col = lax.broadcasted_iota(jnp.int32, (MIX_ROWS, MIX_ROWS), 1)
    diff = (row - col).astype(F32)
    for h in range(RET_HEADS):
        lg = logg_ref[h]
        intra_ref[h] = jnp.where(diff >= 0, jnp.exp(lg * jnp.maximum(diff, 0.0)), 0.0) * key_scale
        qdec_ref[h] = jnp.exp(lg * (pos + 1.0))
        kdec_ref[h] = jnp.exp(lg * (MIX_ROWS - 1.0 - pos)) * key_scale


def _ret_step(logg_ref, q_ref, k_ref, v_ref, g_ref, norm_ref, o_ref,
              state_ref, intra_ref, qdec_ref, kdec_ref):
    for h in range(RET_HEADS):
        k_cols = slice(h * RET_KEY_DIM, (h + 1) * RET_KEY_DIM)
        v_cols = slice(h * RET_VALUE_DIM, (h + 1) * RET_VALUE_DIM)
        chunk_decay = jnp.exp(jnp.full((1, 1), MIX_ROWS, F32) * logg_ref[h])
        q = q_ref[:, k_cols]
        k = k_ref[:, k_cols]
        v = v_ref[:, v_cols]
        state = state_ref[h]
        inner = lax.dot_general(q, k, (((1,), (1,)), ((), ())), preferred_element_type=F32) * intra_ref[h]
        qd = (q.astype(F32) * qdec_ref[h]).astype(BF16)
        o = (jnp.dot(inner.astype(BF16), v, preferred_element_type=F32)
             + jnp.dot(qd, state.astype(BF16), preferred_element_type=F32))
        kd = (k.astype(F32) * kdec_ref[h]).astype(BF16)
        state_ref[h] = state * chunk_decay + lax.dot_general(
            kd, v, (((0,), (0,)), ((), ())), preferred_element_type=F32)

        o = o * _rms_scale(o) * norm_ref[:, v_cols]
        g = g_ref[:, v_cols].astype(F32)
        o_ref[:, v_cols] = (g * jax.nn.sigmoid(g) * o).astype(o_ref.dtype)


def _mix_kernel(logg_ref, h_ref, pre_ref, wgate_ref,
                rq_ref, rk_ref, rv_ref, rg_ref, norm_ref,
                sink_ref, aq_ref, kc_ref, kp_ref, vc_ref, vp_ref,
                xc_ref, xp_ref, pw_ref, pscale_ref, round_ref,
                gates_ref, rt_ref, at_ref, pm_ref, rounded_ref,
                xn_ref, state_ref, intra_ref, qdec_ref, kdec_ref,
                bias_ref, s_ref, e_ref, denom_ref):
    rounded_ref[...] = round_ref[...].astype(BF16)
    i = pl.program_id(0)
    j = pl.program_id(1)
    seq_block = (i * MIX_STEPS + j) % (SEQ // MIX_ROWS)

    @pl.when((i == 0) & (j == 0))
    def _():
        _ret_init_tables(logg_ref, intra_ref, qdec_ref, kdec_ref)
        _attn_init_bias(bias_ref)

    @pl.when(seq_block == 0)
    def _():
        state_ref[...] = jnp.zeros_like(state_ref)

    def step():
        gates_ref[...] = jnp.dot(xn_ref[...], wgate_ref[...],
                                 preferred_element_type=F32).astype(gates_ref.dtype)
        _ret_step(logg_ref, rq_ref, rk_ref, rv_ref, rg_ref, norm_ref, rt_ref,
                  state_ref, intra_ref, qdec_ref, kdec_ref)
        _attn_step(sink_ref, aq_ref, kc_ref, kp_ref, vc_ref, vp_ref, at_ref,
                   bias_ref, s_ref, e_ref, denom_ref, seq_block)
        _pool_step(xc_ref, xp_ref, pw_ref, pscale_ref, pm_ref, seq_block)

    @pl.when(j == 0)
    def _():
        x = h_ref[...]
        xn_ref[...] = (x * _rms_scale(x) * pre_ref[...]).astype(BF16)
        step()

    pl.when(j > 0)(step)


def _mix(h, pre, w_gate, p, log_g, ret_norm, sinks, pool_w, pool_scale, round_w, layer):
    def row(i, j):
        return i * MIX_STEPS + j

    def prev_row(i, j):
        return jnp.maximum(row(i, j) - 1, 0)

    def slab(width, offset):
        return pl.BlockSpec((MIX_ROWS, width), lambda i, j: (row(i, j), offset // width))

    halo_blocks = MIX_ROWS // POOL_HALO
    kcol = OFF_K // KV_WIDTH
    vcol = OFF_V // KV_WIDTH
    in_specs = [
        pl.BlockSpec(memory_space=pltpu.SMEM),
        pl.BlockSpec((MIX_TM, D_MODEL), lambda i, j: (i, 0)),
        pl.BlockSpec((None, 1, D_MODEL), lambda i, j: (layer, 0, 0)),
        pl.BlockSpec((D_MODEL, GATE_TN), lambda i, j: (0, j)),
        slab(RQK_WIDTH, OFF_RQ), slab(RQK_WIDTH, OFF_RK), slab(RV_WIDTH, OFF_RV), slab(RV_WIDTH, OFF_RG),
        pl.BlockSpec((None, 1, RV_WIDTH), lambda i, j: (layer, 0, 0)),
        pl.BlockSpec((None, ATTN_HEADS, 1, 1), lambda i, j: (layer, 0, 0, 0)),
        slab(Q_WIDTH, OFF_Q),
        pl.BlockSpec((MIX_ROWS, KV_WIDTH), lambda i, j: (row(i, j), kcol)),
        pl.BlockSpec((MIX_ROWS, KV_WIDTH), lambda i, j: (prev_row(i, j), kcol)),
        pl.BlockSpec((MIX_ROWS, KV_WIDTH), lambda i, j: (row(i, j), vcol)),
        pl.BlockSpec((MIX_ROWS, KV_WIDTH), lambda i, j: (prev_row(i, j), vcol)),
        slab(POOL_WIDTH, OFF_XP),
        pl.BlockSpec((POOL_HALO, POOL_WIDTH),
                     lambda i, j: (jnp.maximum(row(i, j) * halo_blocks - 1, 0), 0)),
        pl.BlockSpec((None, len(POOL_WINDOWS), POOL_GROUP, POOL_GROUP), lambda i, j: (layer, 0, 0, 0)),
        pl.BlockSpec((None, 1, POOL_WIDTH), lambda i, j: (layer, 0, 0)),
    ]
    round_in_specs, round_out_specs, round_shapes = _round_specs((round_w,), layer, row)
    return pl.pallas_call(
        _mix_kernel,
        grid=(TOKENS // MIX_TM, MIX_STEPS),
        in_specs=in_specs + round_in_specs,
        out_specs=[
            pl.BlockSpec((MIX_TM, GATE_TN), lambda i, j: (i, j)),
            pl.BlockSpec((MIX_ROWS, RV_WIDTH), lambda i, j: (row(i, j), 0)),
            pl.BlockSpec((MIX_ROWS, Q_WIDTH), lambda i, j: (row(i, j), 0)),
            pl.BlockSpec((MIX_ROWS, POOL_WIDTH), lambda i, j: (row(i, j), 0)),
        ] + round_out_specs,
        out_shape=[
            jax.ShapeDtypeStruct((TOKENS, GATE_WIDTH), BF16),
            jax.ShapeDtypeStruct((TOKENS, RV_WIDTH), BF16),
            jax.ShapeDtypeStruct((TOKENS, Q_WIDTH), BF16),
            jax.ShapeDtypeStruct((TOKENS, POOL_WIDTH), BF16),
        ] + round_shapes,
        scratch_shapes=[
            pltpu.VMEM((MIX_TM, D_MODEL), BF16),
            pltpu.VMEM((RET_HEADS, RET_KEY_DIM, RET_VALUE_DIM), F32),
            pltpu.VMEM((RET_HEADS, MIX_ROWS, MIX_ROWS), F32),
            pltpu.VMEM((RET_HEADS, MIX_ROWS, 1), F32),
            pltpu.VMEM((RET_HEADS, MIX_ROWS, 1), F32),
            pltpu.VMEM((2, ATTN_HEADS, ATTN_BLOCK, ATTN_BLOCK), F32),
            pltpu.VMEM((ATTN_HEADS, ATTN_BLOCK, ATTN_BLOCK), F32),
            pltpu.VMEM((ATTN_HEADS, 2 * ATTN_BLOCK, ATTN_BLOCK), BF16),
            pltpu.VMEM((ATTN_HEADS, 1, ATTN_BLOCK), F32),
        ],
        compiler_params=_params("arbitrary", "arbitrary"),
        name="mix",
    )(log_g, h, pre, w_gate, p, p, p, p, ret_norm, sinks,
      p, p, p, p, p, p, p, pool_w, pool_scale, round_w)


def _merge_kernel(h_ref, pm_ref, at_ref, rt_ref, gates_ref,
                  wp_ref, wa_ref, wr_ref, wo_ref, post_ref, o_ref, acc_ref, scale_ref):
    def gate(branch, cols):
        return jax.nn.sigmoid(gates_ref[:, branch * D_MODEL + cols.start:branch * D_MODEL + cols.stop]
                              .astype(F32))

    for jt in range(D_MODEL // MERGE_TJ):
        cols = slice(jt * MERGE_TJ, (jt + 1) * MERGE_TJ)
        merged = (gate(0, cols) * jnp.dot(pm_ref[...], wp_ref[:, cols], preferred_element_type=F32)
                  + gate(1, cols) * jnp.dot(at_ref[...], wa_ref[:, cols], preferred_element_type=F32)
                  + gate(2, cols) * jnp.dot(rt_ref[...], wr_ref[:, cols], preferred_element_type=F32))
        part = jnp.dot(merged.astype(BF16), wo_ref[cols, :], preferred_element_type=F32)
        if jt == 0:
            acc_ref[...] = part
        else:
            acc_ref[...] += part

    scale_ref[...] = _rms_scale(acc_ref[...])
    o_ref[...] = h_ref[...] + acc_ref[...] * scale_ref[...] * post_ref[...]


def _merge(h, pm, at, rt, gates, w_pool_out, w_attn_out, w_ret_out, w_out, post, layer):
    def resident(rows, cols):
        return pl.BlockSpec((rows, cols), lambda i: (0, 0), pipeline_mode=pl.Buffered(1))

    return pl.pallas_call(
        _merge_kernel,
        grid=(TOKENS // MERGE_TM,),
        in_specs=[
            pl.BlockSpec((MERGE_TM, D_MODEL), lambda i: (i, 0)),
            pl.BlockSpec((MERGE_TM, POOL_WIDTH), lambda i: (i, 0)),
            pl.BlockSpec((MERGE_TM, Q_WIDTH), lambda i: (i, 0)),
            pl.BlockSpec((MERGE_TM, RV_WIDTH), lambda i: (i, 0)),
            pl.BlockSpec((MERGE_TM, GATE_WIDTH), lambda i: (i, 0)),
            resident(POOL_WIDTH, D_MODEL),
            resident(Q_WIDTH, D_MODEL),
            resident(RV_WIDTH, D_MODEL),
            resident(D_MODEL, D_MODEL),
            pl.BlockSpec((None, 1, D_MODEL), lambda i: (layer, 0, 0)),
        ],
        out_specs=pl.BlockSpec((MERGE_TM, D_MODEL), lambda i: (i, 0)),
        out_shape=jax.ShapeDtypeStruct((TOKENS, D_MODEL), F32),
        scratch_shapes=[pltpu.VMEM((MERGE_TM, D_MODEL), F32), pltpu.VMEM((MERGE_TM, 1), F32)],
        compiler_params=_params("parallel"),
        name="merge",
    )(h, pm, at, rt, gates, w_pool_out, w_attn_out, w_ret_out, w_out, post)


def kernel(x, ffn1_pre, ffn1_up, ffn1_down, ffn1_post, mix_pre, w_in, pool_w, pool_scale, attn_sinks, ret_norm, w_pool_out, w_attn_out, w_ret_out, w_gate, w_out, mix_post, ffn2_pre, ffn2_up, ffn2_down, ffn2_post):
    def gain(a):
        return a.reshape(DEPTH, 1, a.shape[-1])

    pool_w = pool_w.astype(BF16)
    log_g = jnp.log(1.0 - jnp.exp2(-5.0 - jnp.arange(RET_HEADS, dtype=F32)))
    sinks = attn_sinks.reshape(DEPTH, ATTN_HEADS, 1, 1)

    def narrow(w_down):
        return w_down.reshape(DEPTH, 2 * FFN_HIDDEN, D_MODEL // 2)

    def widen(w_down_bf):
        return w_down_bf.reshape(FFN_HIDDEN, D_MODEL)

    h = x.reshape(TOKENS, D_MODEL)
    down1_bf = None
    for l in range(DEPTH):
        h, (w_gate_bf, w_pool_bf, w_attn_bf, w_ret_bf, w_out_bf) = _ffn(
            h, gain(ffn1_pre), ffn1_up, ffn1_down, gain(ffn1_post), l,
            round_also=(w_gate, w_pool_out, w_attn_out, w_ret_out, w_out), down_bf=down1_bf)
        p = _proj(h, gain(mix_pre), w_in, l)
        gates, rt, at, pm, down2_bf = _mix(h, gain(mix_pre), w_gate_bf, p, log_g, gain(ret_norm), sinks,
                                           pool_w, gain(pool_scale), narrow(ffn2_down), l)
        h = _merge(h, pm, at, rt, gates, w_pool_bf, w_attn_bf, w_ret_bf, w_out_bf, gain(mix_post), l)
        next_down = (narrow(ffn1_down),) if l + 1 < DEPTH else ()
        h, rounded = _ffn(h, gain(ffn2_pre), ffn2_up, ffn2_down, gain(ffn2_post), l,
                          round_also=next_down, round_layer=l + 1, down_bf=widen(down2_bf))
        down1_bf = widen(rounded[0]) if next_down else None
    return h.reshape(BATCH, SEQ, D_MODEL)
```

```python
import functools

import jax
import jax.numpy as jnp
from jax import lax
from jax.experimental import pallas as pl
from jax.experimental.pallas import tpu as pltpu

F32 = jnp.float32
BF16 = jnp.bfloat16

D_MODEL = 2048
BATCH = 4
SEQ = 2048
DEPTH = 2
TOKENS = BATCH * SEQ

POOL_WINDOWS = (2, 4, 8, 16)
POOL_GROUP = 256
POOL_WIDTH = 1024
POOL_HALO = 16
ATTN_HEADS = 16
ATTN_KV_HEADS = 4
ATTN_GROUP = ATTN_HEADS // ATTN_KV_HEADS
ATTN_HEAD_DIM = 64
ATTN_BLOCK = 128
Q_WIDTH = 1024
KV_WIDTH = 256
RET_HEADS = 4
RET_KEY_DIM = 256
RET_VALUE_DIM = 512
RQK_WIDTH = 1024
RV_WIDTH = 2048
FFN_HIDDEN = 5632
RMS_EPS = 1e-6
GATE_WIDTH = 3 * D_MODEL

OFF_XP = 0
OFF_Q = OFF_XP + POOL_WIDTH
OFF_RQ = OFF_Q + Q_WIDTH
OFF_RK = OFF_RQ + RQK_WIDTH
OFF_RV = OFF_RK + RQK_WIDTH
OFF_RG = OFF_RV + RV_WIDTH
OFF_K = OFF_RG + RV_WIDTH
OFF_V = OFF_K + KV_WIDTH
IN_WIDTH = OFF_V + KV_WIDTH
W_IN_KV_OFFSET = POOL_WIDTH + Q_WIDTH

VMEM_LIMIT_BYTES = 60 * 1024 * 1024

FFN_TM = 1024
FFN_TF = 256
FFN_TAIL_TF = 512
ROUND_BLOCKS = 64
PROJ_TM = 2048
PROJ_TN = 512
MIX_TM = 1024
MIX_ROWS = ATTN_BLOCK
MIX_STEPS = MIX_TM // MIX_ROWS
GATE_TN = GATE_WIDTH // MIX_STEPS
MERGE_TM = 256
MERGE_TJ = 1024


def _params(*sem):
    return pltpu.CompilerParams(dimension_semantics=sem, vmem_limit_bytes=VMEM_LIMIT_BYTES)


def _rms_scale(x):
    return lax.rsqrt(jnp.mean(x * x, axis=-1, keepdims=True) + RMS_EPS)


def _ffn_steps(h_ref, pre_ref, wgu_ref, wd_ref, post_ref, o_ref, xn_ref, scale_ref):
    j = pl.program_id(1)
    n_sub = wgu_ref.shape[-1] // (2 * FFN_TF)

    def hidden_tile_step(first):
        gu = jnp.dot(xn_ref[...], wgu_ref[...], preferred_element_type=F32)
        parts = [gu[:, k * FFN_TF:(k + 1) * FFN_TF] for k in range(2 * n_sub)]
        g = jnp.concatenate(parts[0::2], axis=1)
        u = jnp.concatenate(parts[1::2], axis=1)
        a = (g * jax.nn.sigmoid(g) * u).astype(BF16)
        part = jnp.dot(a, wd_ref[...], preferred_element_type=F32)
        if first:
            o_ref[...] = part
        else:
            o_ref[...] += part

    @pl.when(j == 0)
    def _():
        x = h_ref[...]
        xn_ref[...] = (x * _rms_scale(x) * pre_ref[...]).astype(BF16)
        hidden_tile_step(first=True)

    @pl.when(j > 0)
    def _():
        hidden_tile_step(first=False)

    @pl.when(j == pl.num_programs(1) - 1)
    def _():
        scale_ref[...] = _rms_scale(o_ref[...])
        o_ref[...] = h_ref[...] + 0.5 * (o_ref[...] * scale_ref[...] * post_ref[...])


def _ffn_head_kernel(h_ref, pre_ref, wg_ref, wu_ref, wd_ref, post_ref,
                     o_ref, wgu_bf_ref, wd_bf_ref, xn_ref, scale_ref):
    wgu_bf_ref[...] = jnp.concatenate([wg_ref[...].astype(BF16), wu_ref[...].astype(BF16)], axis=1)
    wd_bf_ref[...] = wd_ref[...].astype(BF16)
    _ffn_steps(h_ref, pre_ref, wgu_bf_ref, wd_bf_ref, post_ref, o_ref, xn_ref, scale_ref)


def _ffn_tail_kernel(*refs, n_round):
    head_out_ref, h_ref, pre_ref, wgu_ref, wd_ref, post_ref = refs[:6]
    round_in = refs[6:6 + n_round]
    o_ref = refs[6 + n_round]
    round_out = refs[7 + n_round:7 + 2 * n_round]
    xn_ref, scale_ref = refs[7 + 2 * n_round:]
    del head_out_ref
    for src, dst in zip(round_in, round_out):
        dst[...] = src[...].astype(BF16)
    _ffn_steps(h_ref, pre_ref, wgu_ref, wd_ref, post_ref, o_ref, xn_ref, scale_ref)


def _ffn(h, pre, w_up, w_down, post, layer, round_also=()):
    nf = FFN_HIDDEN // FFN_TF
    gain_spec = pl.BlockSpec((None, 1, D_MODEL), lambda i, j: (layer, 0, 0))
    scratch = [pltpu.VMEM((FFN_TM, D_MODEL), BF16), pltpu.VMEM((FFN_TM, 1), F32)]
    out, wgu_bf, wd_bf = pl.pallas_call(
        _ffn_head_kernel,
        grid=(1, nf),
        in_specs=[
            pl.BlockSpec((FFN_TM, D_MODEL), lambda i, j: (0, 0)),
            gain_spec,
            pl.BlockSpec((None, D_MODEL, FFN_TF), lambda i, j: (layer, 0, j)),
            pl.BlockSpec((None, D_MODEL, FFN_TF), lambda i, j: (layer, 0, nf + j)),
            pl.BlockSpec((None, FFN_TF, D_MODEL), lambda i, j: (layer, j, 0)),
            gain_spec,
        ],
        out_specs=[
            pl.BlockSpec((FFN_TM, D_MODEL), lambda i, j: (0, 0)),
            pl.BlockSpec((D_MODEL, 2 * FFN_TF), lambda i, j: (0, j)),
            pl.BlockSpec((FFN_TF, D_MODEL), lambda i, j: (j, 0)),
        ],
        out_shape=[
            jax.ShapeDtypeStruct((TOKENS, D_MODEL), F32),
            jax.ShapeDtypeStruct((D_MODEL, 2 * FFN_HIDDEN), BF16),
            jax.ShapeDtypeStruct((FFN_HIDDEN, D_MODEL), BF16),
        ],
        scratch_shapes=scratch,
        compiler_params=_params("arbitrary", "arbitrary"),
        name="ffn_head",
    )(h, pre, w_up, w_up, w_down, post)

    nf2 = FFN_HIDDEN // FFN_TAIL_TF
    n_tiles = TOKENS // FFN_TM - 1
    assert ROUND_BLOCKS <= n_tiles * nf2

    def round_block(i, j):
        return jnp.minimum(i * nf2 + j, ROUND_BLOCKS - 1)

    round_in_specs, round_out_specs, round_shapes = [], [], []
    for w in round_also:
        rows, cols = w.shape[1:]
        block_rows = rows // ROUND_BLOCKS
        round_in_specs.append(pl.BlockSpec((None, block_rows, cols), lambda i, j: (layer, round_block(i, j), 0)))
        round_out_specs.append(pl.BlockSpec((block_rows, cols), lambda i, j: (round_block(i, j), 0)))
        round_shapes.append(jax.ShapeDtypeStruct((rows, cols), BF16))

    out, *rounded = pl.pallas_call(
        functools.partial(_ffn_tail_kernel, n_round=len(round_also)),
        grid=(n_tiles, nf2),
        in_specs=[
            pl.BlockSpec(memory_space=pl.ANY),
            pl.BlockSpec((FFN_TM, D_MODEL), lambda i, j: (i + 1, 0)),
            gain_spec,
            pl.BlockSpec((D_MODEL, 2 * FFN_TAIL_TF), lambda i, j: (0, j)),
            pl.BlockSpec((FFN_TAIL_TF, D_MODEL), lambda i, j: (j, 0)),
            gain_spec,
        ] + round_in_specs,
        out_specs=[pl.BlockSpec((FFN_TM, D_MODEL), lambda i, j: (i + 1, 0))] + round_out_specs,
        out_shape=[jax.ShapeDtypeStruct((TOKENS, D_MODEL), F32)] + round_shapes,
        input_output_aliases={0: 0},
        scratch_shapes=scratch,
        compiler_params=_params("arbitrary", "arbitrary"),
        name="ffn_tail",
    )(out, h, pre, wgu_bf, wd_bf, post, *round_also)
    return out, rounded


def _proj_kernel(h_ref, pre_ref, w_ref, o_ref, xn_ref):
    @pl.when(pl.program_id(1) == 0)
    def _():
        x = h_ref[...]
        xn_ref[...] = (x * _rms_scale(x) * pre_ref[...]).astype(BF16)

    o_ref[...] = jnp.dot(xn_ref[...], w_ref[...].astype(BF16),
                         preferred_element_type=F32).astype(o_ref.dtype)


def _proj(h, pre, w_in, layer):
    kv_tile = W_IN_KV_OFFSET // PROJ_TN
    last_tile = IN_WIDTH // PROJ_TN - 1

    def w_tile(j):
        return jnp.where(j < kv_tile, j, jnp.where(j < last_tile, j + 1, kv_tile))

    return pl.pallas_call(
        _proj_kernel,
        grid=(TOKENS // PROJ_TM, IN_WIDTH // PROJ_TN),
        in_specs=[
            pl.BlockSpec((PROJ_TM, D_MODEL), lambda i, j: (i, 0)),
            pl.BlockSpec((None, 1, D_MODEL), lambda i, j: (layer, 0, 0)),
            pl.BlockSpec((None, D_MODEL, PROJ_TN), lambda i, j: (layer, 0, w_tile(j))),
        ],
        out_specs=pl.BlockSpec((PROJ_TM, PROJ_TN), lambda i, j: (i, j)),
        out_shape=jax.ShapeDtypeStruct((TOKENS, IN_WIDTH), BF16),
        scratch_shapes=[pltpu.VMEM((PROJ_TM, D_MODEL), BF16)],
        compiler_params=_params("parallel", "arbitrary"),
        name="proj",
    )(h, pre, w_in)


def _pool_step(xc_ref, xp_ref, pw_ref, scale_ref, o_ref, seq_block):
    cur = xc_ref[...].astype(F32)
    prev = jnp.where(seq_block == 0, 0.0, xp_ref[...].astype(F32))
    ext = jnp.concatenate([prev, cur], axis=0)
    t = seq_block * MIX_ROWS + lax.broadcasted_iota(jnp.int32, (MIX_ROWS, 1), 0)
    for gi, w in enumerate(POOL_WINDOWS):
        cols = slice(gi * POOL_GROUP, (gi + 1) * POOL_GROUP)
        s = ext[:, cols]
        span = 1
        while span < w:
            s = s + pltpu.roll(s, span, 0)
            span *= 2
        wsum = s[POOL_HALO:, :]
        count = jnp.minimum(t + 1, w).astype(F32)
        pooled = wsum / count - cur[:, cols]
        mixed = jnp.dot(pooled.astype(BF16), pw_ref[gi], preferred_element_type=F32)
        o_ref[:, cols] = (mixed * scale_ref[:, cols]).astype(o_ref.dtype)


def _attn_fold_mask():
    ci = lax.broadcasted_iota(jnp.int32, (ATTN_BLOCK, ATTN_BLOCK), 0)
    qi = lax.broadcasted_iota(jnp.int32, (ATTN_BLOCK, ATTN_BLOCK), 1)
    return ci, qi, ci > qi


def _attn_init_bias(bias_ref):
    ci, qi, from_prev = _attn_fold_mask()
    dist = jnp.where(from_prev, qi + ATTN_BLOCK - ci, qi - ci).astype(F32)
    for h in range(ATTN_HEADS):
        alibi = -(2.0 ** (-8.0 * (h + 1) / ATTN_HEADS)) * dist
        bias_ref[0, h] = alibi
        bias_ref[1, h] = jnp.where(from_prev, -jnp.inf, alibi)


def _attn_step(sink_ref, q_ref, kc_ref, kp_ref, vc_ref, vp_ref, o_ref,
               bias_ref, s_ref, e_ref, denom_ref, seq_block):
    _, _, from_prev = _attn_fold_mask()
    seq_start = (seq_block == 0).astype(jnp.int32)
    q = q_ref[...] * (ATTN_HEAD_DIM ** -0.5)
    kk = jnp.concatenate([kp_ref[...], kc_ref[...]], axis=0)
    vv = jnp.concatenate([vp_ref[...], vc_ref[...]], axis=0)
    vv_t = vv.astype(F32).T.astype(BF16)

    for h in range(ATTN_HEADS):
        kv_cols = slice((h // ATTN_GROUP) * ATTN_HEAD_DIM, (h // ATTN_GROUP + 1) * ATTN_HEAD_DIM)
        s2 = lax.dot_general(kk[:, kv_cols], q[:, h * ATTN_HEAD_DIM:(h + 1) * ATTN_HEAD_DIM],
                             (((1,), (1,)), ((), ())), preferred_element_type=F32)
        s_ref[h] = jnp.where(from_prev, s2[:ATTN_BLOCK], s2[ATTN_BLOCK:]) + bias_ref[seq_start, h]

    s = s_ref[...]
    sink = sink_ref[...]
    m = jnp.maximum(jnp.max(s, axis=1, keepdims=True), sink)
    e = jnp.exp(s - m)
    denom_ref[...] = jnp.sum(e, axis=1, keepdims=True) + jnp.exp(sink - m)
    eb = e.astype(BF16)
    zero = jnp.zeros_like(eb)
    e_ref[:, :ATTN_BLOCK, :] = jnp.where(from_prev, eb, zero)
    e_ref[:, ATTN_BLOCK:, :] = jnp.where(from_prev, zero, eb)

    for pair in range(ATTN_HEADS // 2):
        outs = []
        for h in (2 * pair, 2 * pair + 1):
            kv_rows = slice((h // ATTN_GROUP) * ATTN_HEAD_DIM, (h // ATTN_GROUP + 1) * ATTN_HEAD_DIM)
            pv_t = jnp.dot(vv_t[kv_rows], e_ref[h], preferred_element_type=F32)
            outs.append(pv_t / denom_ref[h])
        pair_cols = slice(2 * pair * ATTN_HEAD_DIM, (2 * pair + 2) * ATTN_HEAD_DIM)
        o_ref[:, pair_cols] = jnp.concatenate(outs, axis=0).T.astype(o_ref.dtype)


def _ret_init_tables(logg_ref, intra_ref, qdec_ref, kdec_ref):
    key_scale = RET_KEY_DIM ** -0.5
    pos = lax.broadcasted_iota(jnp.int32, (MIX_ROWS, 1), 0).astype(F32)
    row = lax.broadcasted_iota(jnp.int32, (MIX_ROWS, MIX_ROWS), 0)
    col = lax.broadcasted_iota(jnp.int32, (MIX_ROWS, MIX_ROWS), 1)
    diff = (row - col).astype(F32)
    for h in range(RET_HEADS):
        lg = logg_ref[h]
        intra_ref[h] = jnp.where(diff >= 0, jnp.exp(lg * jnp.maximum(diff, 0.0)), 0.0) * key_scale
        qdec_ref[h] = jnp.exp(lg * (pos + 1.0))
        kdec_ref[h] = jnp.exp(lg * (MIX_ROWS - 1.0 - pos)) * key_scale


def _ret_step(logg_ref, q_ref, k_ref, v_ref, g_ref, norm_ref, o_ref,
              state_ref, intra_ref, qdec_ref, kdec_ref):
    for h in range(RET_HEADS):
        k_cols = slice(h * RET_KEY_DIM, (h + 1) * RET_KEY_DIM)
        v_cols = slice(h * RET_VALUE_DIM, (h + 1) * RET_VALUE_DIM)
        chunk_decay = jnp.exp(jnp.full((1, 1), MIX_ROWS, F32) * logg_ref[h])
        q = q_ref[:, k_cols]
        k = k_ref[:, k_cols]
        v = v_ref[:, v_cols]
        state = state_ref[h]
        inner = lax.dot_general(q, k, (((1,), (1,)), ((), ())), preferred_element_type=F32) * intra_ref[h]
        qd = (q.astype(F32) * qdec_ref[h]).astype(BF16)
        o = (jnp.dot(inner.astype(BF16), v, preferred_element_type=F32)
             + jnp.dot(qd, state.astype(BF16), preferred_element_type=F32))
        kd = (k.astype(F32) * kdec_ref[h]).astype(BF16)
        state_ref[h] = state * chunk_decay + lax.dot_general(
            kd, v, (((0,), (0,)), ((), ())), preferred_element_type=F32)

        o = o * _rms_scale(o) * norm_ref[:, v_cols]
        g = g_ref[:, v_cols].astype(F32)
        o_ref[:, v_cols] = (g * jax.nn.sigmoid(g) * o).astype(o_ref.dtype)


def _mix_kernel(logg_ref, h_ref, pre_ref, wgate_ref,
                rq_ref, rk_ref, rv_ref, rg_ref, norm_ref,
                sink_ref, aq_ref, kc_ref, kp_ref, vc_ref, vp_ref,
                xc_ref, xp_ref, pw_ref, pscale_ref,
                gates_ref, rt_ref, at_ref, pm_ref,
                xn_ref, state_ref, intra_ref, qdec_ref, kdec_ref,
                bias_ref, s_ref, e_ref, denom_ref):
    i = pl.program_id(0)
    j = pl.program_id(1)
    seq_block = (i * MIX_STEPS + j) % (SEQ // MIX_ROWS)

    @pl.when((i == 0) & (j == 0))
    def _():
        _ret_init_tables(logg_ref, intra_ref, qdec_ref, kdec_ref)
        _attn_init_bias(bias_ref)

    @pl.when(seq_block == 0)
    def _():
        state_ref[...] = jnp.zeros_like(state_ref)

    def step():
        gates_ref[...] = jnp.dot(xn_ref[...], wgate_ref[...],
                                 preferred_element_type=F32).astype(gates_ref.dtype)
        _ret_step(logg_ref, rq_ref, rk_ref, rv_ref, rg_ref, norm_ref, rt_ref,
                  state_ref, intra_ref, qdec_ref, kdec_ref)
        _attn_step(sink_ref, aq_ref, kc_ref, kp_ref, vc_ref, vp_ref, at_ref,
                   bias_ref, s_ref, e_ref, denom_ref, seq_block)
        _pool_step(xc_ref, xp_ref, pw_ref, pscale_ref, pm_ref, seq_block)

    @pl.when(j == 0)
    def _():
        x = h_ref[...]
        xn_ref[...] = (x * _rms_scale(x) * pre_ref[...]).astype(BF16)
        step()

    pl.when(j > 0)(step)


def _mix(h, pre, w_gate, p, log_g, ret_norm, sinks, pool_w, pool_scale, layer):
    def row(i, j):
        return i * MIX_STEPS + j

    def prev_row(i, j):
        return jnp.maximum(row(i, j) - 1, 0)

    def slab(width, offset):
        return pl.BlockSpec((MIX_ROWS, width), lambda i, j: (row(i, j), offset // width))

    halo_blocks = MIX_ROWS // POOL_HALO
    kcol = OFF_K // KV_WIDTH
    vcol = OFF_V // KV_WIDTH
    in_specs = [
        pl.BlockSpec(memory_space=pltpu.SMEM),
        pl.BlockSpec((MIX_TM, D_MODEL), lambda i, j: (i, 0)),
        pl.BlockSpec((None, 1, D_MODEL), lambda i, j: (layer, 0, 0)),
        pl.BlockSpec((D_MODEL, GATE_TN), lambda i, j: (0, j)),
        slab(RQK_WIDTH, OFF_RQ), slab(RQK_WIDTH, OFF_RK), slab(RV_WIDTH, OFF_RV), slab(RV_WIDTH, OFF_RG),
        pl.BlockSpec((None, 1, RV_WIDTH), lambda i, j: (layer, 0, 0)),
        pl.BlockSpec((None, ATTN_HEADS, 1, 1), lambda i, j: (layer, 0, 0, 0)),
        slab(Q_WIDTH, OFF_Q),
        pl.BlockSpec((MIX_ROWS, KV_WIDTH), lambda i, j: (row(i, j), kcol)),
        pl.BlockSpec((MIX_ROWS, KV_WIDTH), lambda i, j: (prev_row(i, j), kcol)),
        pl.BlockSpec((MIX_ROWS, KV_WIDTH), lambda i, j: (row(i, j), vcol)),
        pl.BlockSpec((MIX_ROWS, KV_WIDTH), lambda i, j: (prev_row(i, j), vcol)),
        slab(POOL_WIDTH, OFF_XP),
        pl.BlockSpec((POOL_HALO, POOL_WIDTH),
                     lambda i, j: (jnp.maximum(row(i, j) * halo_blocks - 1, 0), 0)),
        pl.BlockSpec((None, len(POOL_WINDOWS), POOL_GROUP, POOL_GROUP), lambda i, j: (layer, 0, 0, 0)),
        pl.BlockSpec((None, 1, POOL_WIDTH), lambda i, j: (layer, 0, 0)),
    ]
    return pl.pallas_call(
        _mix_kernel,
        grid=(TOKENS // MIX_TM, MIX_STEPS),
        in_specs=in_specs,
        out_specs=[
            pl.BlockSpec((MIX_TM, GATE_TN), lambda i, j: (i, j)),
            pl.BlockSpec((MIX_ROWS, RV_WIDTH), lambda i, j: (row(i, j), 0)),
            pl.BlockSpec((MIX_ROWS, Q_WIDTH), lambda i, j: (row(i, j), 0)),
            pl.BlockSpec((MIX_ROWS, POOL_WIDTH), lambda i, j: (row(i, j), 0)),
        ],
        out_shape=[
            jax.ShapeDtypeStruct((TOKENS, GATE_WIDTH), BF16),
            jax.ShapeDtypeStruct((TOKENS, RV_WIDTH), BF16),
            jax.ShapeDtypeStruct((TOKENS, Q_WIDTH), BF16),
            jax.ShapeDtypeStruct((TOKENS, POOL_WIDTH), BF16),
        ],
        scratch_shapes=[
            pltpu.VMEM((MIX_TM, D_MODEL), BF16),
            pltpu.VMEM((RET_HEADS, RET_KEY_DIM, RET_VALUE_DIM), F32),
            pltpu.VMEM((RET_HEADS, MIX_ROWS, MIX_ROWS), F32),
            pltpu.VMEM((RET_HEADS, MIX_ROWS, 1), F32),
            pltpu.VMEM((RET_HEADS, MIX_ROWS, 1), F32),
            pltpu.VMEM((2, ATTN_HEADS, ATTN_BLOCK, ATTN_BLOCK), F32),
            pltpu.VMEM((ATTN_HEADS, ATTN_BLOCK, ATTN_BLOCK), F32),
            pltpu.VMEM((ATTN_HEADS, 2 * ATTN_BLOCK, ATTN_BLOCK), BF16),
            pltpu.VMEM((ATTN_HEADS, 1, ATTN_BLOCK), F32),
        ],
        compiler_params=_params("arbitrary", "arbitrary"),
        name="mix",
    )(log_g, h, pre, w_gate, p, p, p, p, ret_norm, sinks,
      p, p, p, p, p, p, p, pool_w, pool_scale)


def _merge_kernel(h_ref, pm_ref, at_ref, rt_ref, gates_ref,
                  wp_ref, wa_ref, wr_ref, wo_ref, post_ref, o_ref, acc_ref, scale_ref):
    def gate(branch, cols):
        return jax.nn.sigmoid(gates_ref[:, branch * D_MODEL + cols.start:branch * D_MODEL + cols.stop]
                              .astype(F32))

    for jt in range(D_MODEL // MERGE_TJ):
        cols = slice(jt * MERGE_TJ, (jt + 1) * MERGE_TJ)
        merged = (gate(0, cols) * jnp.dot(pm_ref[...], wp_ref[:, cols], preferred_element_type=F32)
                  + gate(1, cols) * jnp.dot(at_ref[...], wa_ref[:, cols], preferred_element_type=F32)
                  + gate(2, cols) * jnp.dot(rt_ref[...], wr_ref[:, cols], preferred_element_type=F32))
        part = jnp.dot(merged.astype(BF16), wo_ref[cols, :], preferred_element_type=F32)
        if jt == 0:
            acc_ref[...] = part
        else:
            acc_ref[...] += part

    scale_ref[...] = _rms_scale(acc_ref[...])
    o_ref[...] = h_ref[...] + acc_ref[...] * scale_ref[...] * post_ref[...]


def _merge(h, pm, at, rt, gates, w_pool_out, w_attn_out, w_ret_out, w_out, post, layer):
    def resident(rows, cols):
        return pl.BlockSpec((rows, cols), lambda i: (0, 0), pipeline_mode=pl.Buffered(1))

    return pl.pallas_call(
        _merge_kernel,
        grid=(TOKENS // MERGE_TM,),
        in_specs=[
            pl.BlockSpec((MERGE_TM, D_MODEL), lambda i: (i, 0)),
            pl.BlockSpec((MERGE_TM, POOL_WIDTH), lambda i: (i, 0)),
            pl.BlockSpec((MERGE_TM, Q_WIDTH), lambda i: (i, 0)),
            pl.BlockSpec((MERGE_TM, RV_WIDTH), lambda i: (i, 0)),
            pl.BlockSpec((MERGE_TM, GATE_WIDTH), lambda i: (i, 0)),
            resident(POOL_WIDTH, D_MODEL),
            resident(Q_WIDTH, D_MODEL),
            resident(RV_WIDTH, D_MODEL),
            resident(D_MODEL, D_MODEL),
            pl.BlockSpec((None, 1, D_MODEL), lambda i: (layer, 0, 0)),
        ],
        out_specs=pl.BlockSpec((MERGE_TM, D_MODEL), lambda i: (i, 0)),
        out_shape=jax.ShapeDtypeStruct((TOKENS, D_MODEL), F32),
        scratch_shapes=[pltpu.VMEM((MERGE_TM, D_MODEL), F32), pltpu.VMEM((MERGE_TM, 1), F32)],
        compiler_params=_params("parallel"),
        name="merge",
    )(h, pm, at, rt, gates, w_pool_out, w_attn_out, w_ret_out, w_out, post)


def kernel(x, ffn1_pre, ffn1_up, ffn1_down, ffn1_post, mix_pre, w_in, pool_w, pool_scale, attn_sinks, ret_norm, w_pool_out, w_attn_out, w_ret_out, w_gate, w_out, mix_post, ffn2_pre, ffn2_up, ffn2_down, ffn2_post):
    def gain(a):
        return a.reshape(DEPTH, 1, a.shape[-1])

    pool_w = pool_w.astype(BF16)
    log_g = jnp.log(1.0 - jnp.exp2(-5.0 - jnp.arange(RET_HEADS, dtype=F32)))
    sinks = attn_sinks.reshape(DEPTH, ATTN_HEADS, 1, 1)

    h = x.reshape(TOKENS, D_MODEL)
    for l in range(DEPTH):
        h, (w_gate_bf, w_pool_bf, w_attn_bf, w_ret_bf, w_out_bf) = _ffn(
            h, gain(ffn1_pre), ffn1_up, ffn1_down, gain(ffn1_post), l,
            round_also=(w_gate, w_pool_out, w_attn_out, w_ret_out, w_out))
        p = _proj(h, gain(mix_pre), w_in, l)
        gates, rt, at, pm = _mix(h, gain(mix_pre), w_gate_bf, p, log_g, gain(ret_norm), sinks,
                                 pool_w, gain(pool_scale), l)
        h = _merge(h, pm, at, rt, gates, w_pool_bf, w_attn_bf, w_ret_bf, w_out_bf, gain(mix_post), l)
        h, _ = _ffn(h, gain(ffn2_pre), ffn2_up, ffn2_down, gain(ffn2_post), l)
    return h.reshape(BATCH, SEQ, D_MODEL)
```

```python
import functools

import jax
import jax.numpy as jnp
from jax import lax
from jax.experimental import pallas as pl
from jax.experimental.pallas import tpu as pltpu

F32 = jnp.float32
BF16 = jnp.bfloat16

D_MODEL = 2048
BATCH = 4
SEQ = 2048
DEPTH = 2
TOKENS = BATCH * SEQ

POOL_WINDOWS = (2, 4, 8, 16)
POOL_GROUP = 256
POOL_WIDTH = 1024
POOL_HALO = 16
ATTN_HEADS = 16
ATTN_KV_HEADS = 4
ATTN_GROUP = ATTN_HEADS // ATTN_KV_HEADS
ATTN_HEAD_DIM = 64
ATTN_BLOCK = 128
Q_WIDTH = 1024
KV_WIDTH = 256
RET_HEADS = 4
RET_KEY_DIM = 256
RET_VALUE_DIM = 512
RQK_WIDTH = 1024
RV_WIDTH = 2048
FFN_HIDDEN = 5632
RMS_EPS = 1e-6
GATE_WIDTH = 3 * D_MODEL

OFF_XP = 0
OFF_Q = OFF_XP + POOL_WIDTH
OFF_RQ = OFF_Q + Q_WIDTH
OFF_RK = OFF_RQ + RQK_WIDTH
OFF_RV = OFF_RK + RQK_WIDTH
OFF_RG = OFF_RV + RV_WIDTH
OFF_K = OFF_RG + RV_WIDTH
OFF_V = OFF_K + KV_WIDTH
IN_WIDTH = OFF_V + KV_WIDTH
W_IN_KV_OFFSET = POOL_WIDTH + Q_WIDTH

VMEM_LIMIT_BYTES = 60 * 1024 * 1024

FFN_TM = 1024
FFN_TF = 256
FFN_TAIL_TF = 512
ROUND_BLOCKS = 64
PROJ_TM = 2048
PROJ_TN = 512
MIX_TM = 1024
MIX_ROWS = ATTN_BLOCK
MIX_STEPS = MIX_TM // MIX_ROWS
GATE_TN = GATE_WIDTH // MIX_STEPS
MERGE_TM = 256
MERGE_TJ = 1024


def _params(*sem):
    return pltpu.CompilerParams(dimension_semantics=sem, vmem_limit_bytes=VMEM_LIMIT_BYTES)


def _rms_scale(x):
    return lax.rsqrt(jnp.mean(x * x, axis=-1, keepdims=True) + RMS_EPS)


def _ffn_steps(h_ref, pre_ref, wg_ref, wu_ref, wd_ref, post_ref, o_ref, xn_ref, scale_ref, fused_gate_up):
    j = pl.program_id(1)

    def hidden_tile_step(first):
        if fused_gate_up:
            tf = wg_ref.shape[-1]
            gu = jnp.dot(xn_ref[...], jnp.concatenate([wg_ref[...], wu_ref[...]], axis=1),
                         preferred_element_type=F32)
            g, u = gu[:, :tf], gu[:, tf:]
        else:
            g = jnp.dot(xn_ref[...], wg_ref[...], preferred_element_type=F32)
            u = jnp.dot(xn_ref[...], wu_ref[...], preferred_element_type=F32)
        a = (g * jax.nn.sigmoid(g) * u).astype(BF16)
        part = jnp.dot(a, wd_ref[...], preferred_element_type=F32)
        if first:
            o_ref[...] = part
        else:
            o_ref[...] += part

    @pl.when(j == 0)
    def _():
        x = h_ref[...]
        xn_ref[...] = (x * _rms_scale(x) * pre_ref[...]).astype(BF16)
        hidden_tile_step(first=True)

    last = pl.num_programs(1) - 1

    @pl.when((j > 0) & (j < last))
    def _():
        hidden_tile_step(first=False)

    @pl.when(j == last)
    def _():
        hidden_tile_step(first=False)
        scale_ref[...] = _rms_scale(o_ref[...])
        o_ref[...] = h_ref[...] + 0.5 * (o_ref[...] * scale_ref[...] * post_ref[...])


def _ffn_head_kernel(h_ref, pre_ref, wg_ref, wu_ref, wd_ref, post_ref,
                     o_ref, wg_bf_ref, wu_bf_ref, wd_bf_ref, xn_ref, scale_ref):
    wg_bf_ref[...] = wg_ref[...].astype(BF16)
    wu_bf_ref[...] = wu_ref[...].astype(BF16)
    wd_bf_ref[...] = wd_ref[...].astype(BF16)
    _ffn_steps(h_ref, pre_ref, wg_bf_ref, wu_bf_ref, wd_bf_ref, post_ref, o_ref, xn_ref, scale_ref,
               fused_gate_up=True)


def _ffn_tail_kernel(*refs, n_round):
    head_out_ref, h_ref, pre_ref, wg_ref, wu_ref, wd_ref, post_ref = refs[:7]
    round_in = refs[7:7 + n_round]
    o_ref = refs[7 + n_round]
    round_out = refs[8 + n_round:8 + 2 * n_round]
    xn_ref, scale_ref = refs[8 + 2 * n_round:]
    del head_out_ref
    for src, dst in zip(round_in, round_out):
        dst[...] = src[...].astype(BF16)
    _ffn_steps(h_ref, pre_ref, wg_ref, wu_ref, wd_ref, post_ref, o_ref, xn_ref, scale_ref,
               fused_gate_up=False)


def _ffn(h, pre, w_up, w_down, post, layer, round_also=()):
    nf = FFN_HIDDEN // FFN_TF
    gain_spec = pl.BlockSpec((None, 1, D_MODEL), lambda i, j: (layer, 0, 0))
    scratch = [pltpu.VMEM((FFN_TM, D_MODEL), BF16), pltpu.VMEM((FFN_TM, 1), F32)]
    out, wg_bf, wu_bf, wd_bf = pl.pallas_call(
        _ffn_head_kernel,
        grid=(1, nf),
        in_specs=[
            pl.BlockSpec((FFN_TM, D_MODEL), lambda i, j: (0, 0)),
            gain_spec,
            pl.BlockSpec((None, D_MODEL, FFN_TF), lambda i, j: (layer, 0, j)),
            pl.BlockSpec((None, D_MODEL, FFN_TF), lambda i, j: (layer, 0, nf + j)),
            pl.BlockSpec((None, FFN_TF, D_MODEL), lambda i, j: (layer, j, 0)),
            gain_spec,
        ],
        out_specs=[
            pl.BlockSpec((FFN_TM, D_MODEL), lambda i, j: (0, 0)),
            pl.BlockSpec((D_MODEL, FFN_TF), lambda i, j: (0, j)),
            pl.BlockSpec((D_MODEL, FFN_TF), lambda i, j: (0, j)),
            pl.BlockSpec((FFN_TF, D_MODEL), lambda i, j: (j, 0)),
        ],
        out_shape=[
            jax.ShapeDtypeStruct((TOKENS, D_MODEL), F32),
            jax.ShapeDtypeStruct((D_MODEL, FFN_HIDDEN), BF16),
            jax.ShapeDtypeStruct((D_MODEL, FFN_HIDDEN), BF16),
            jax.ShapeDtypeStruct((FFN_HIDDEN, D_MODEL), BF16),
        ],
        scratch_shapes=scratch,
        compiler_params=_params("arbitrary", "arbitrary"),
        name="ffn_head",
    )(h, pre, w_up, w_up, w_down, post)

    nf2 = FFN_HIDDEN // FFN_TAIL_TF
    n_tiles = TOKENS // FFN_TM - 1
    assert ROUND_BLOCKS <= n_tiles * nf2

    def round_block(i, j):
        return jnp.minimum(i * nf2 + j, ROUND_BLOCKS - 1)

    round_in_specs, round_out_specs, round_shapes = [], [], []
    for w in round_also:
        rows, cols = w.shape[1:]
        block_rows = rows // ROUND_BLOCKS
        round_in_specs.append(pl.BlockSpec((None, block_rows, cols), lambda i, j: (layer, round_block(i, j), 0)))
        round_out_specs.append(pl.BlockSpec((block_rows, cols), lambda i, j: (round_block(i, j), 0)))
        round_shapes.append(jax.ShapeDtypeStruct((rows, cols), BF16))

    out, *rounded = pl.pallas_call(
        functools.partial(_ffn_tail_kernel, n_round=len(round_also)),
        grid=(n_tiles, nf2),
        in_specs=[
            pl.BlockSpec(memory_space=pl.ANY),
            pl.BlockSpec((FFN_TM, D_MODEL), lambda i, j: (i + 1, 0)),
            gain_spec,
            pl.BlockSpec((D_MODEL, FFN_TAIL_TF), lambda i, j: (0, j)),
            pl.BlockSpec((D_MODEL, FFN_TAIL_TF), lambda i, j: (0, j)),
            pl.BlockSpec((FFN_TAIL_TF, D_MODEL), lambda i, j: (j, 0)),
            gain_spec,
        ] + round_in_specs,
        out_specs=[pl.BlockSpec((FFN_TM, D_MODEL), lambda i, j: (i + 1, 0))] + round_out_specs,
        out_shape=[jax.ShapeDtypeStruct((TOKENS, D_MODEL), F32)] + round_shapes,
        input_output_aliases={0: 0},
        scratch_shapes=scratch,
        compiler_params=_params("arbitrary", "arbitrary"),
        name="ffn_tail",
    )(out, h, pre, wg_bf, wu_bf, wd_bf, post, *round_also)
    return out, rounded


def _proj_kernel(h_ref, pre_ref, w_ref, o_ref, xn_ref):
    @pl.when(pl.program_id(1) == 0)
    def _():
        x = h_ref[...]
        xn_ref[...] = (x * _rms_scale(x) * pre_ref[...]).astype(BF16)

    o_ref[...] = jnp.dot(xn_ref[...], w_ref[...].astype(BF16),
                         preferred_element_type=F32).astype(o_ref.dtype)


def _proj(h, pre, w_in, layer):
    kv_tile = W_IN_KV_OFFSET // PROJ_TN
    last_tile = IN_WIDTH // PROJ_TN - 1

    def w_tile(j):
        return jnp.where(j < kv_tile, j, jnp.where(j < last_tile, j + 1, kv_tile))

    return pl.pallas_call(
        _proj_kernel,
        grid=(TOKENS // PROJ_TM, IN_WIDTH // PROJ_TN),
        in_specs=[
            pl.BlockSpec((PROJ_TM, D_MODEL), lambda i, j: (i, 0)),
            pl.BlockSpec((None, 1, D_MODEL), lambda i, j: (layer, 0, 0)),
            pl.BlockSpec((None, D_MODEL, PROJ_TN), lambda i, j: (layer, 0, w_tile(j))),
        ],
        out_specs=pl.BlockSpec((PROJ_TM, PROJ_TN), lambda i, j: (i, j)),
        out_shape=jax.ShapeDtypeStruct((TOKENS, IN_WIDTH), BF16),
        scratch_shapes=[pltpu.VMEM((PROJ_TM, D_MODEL), BF16)],
        compiler_params=_params("parallel", "arbitrary"),
        name="proj",
    )(h, pre, w_in)


def _pool_step(xc_ref, xp_ref, pw_ref, scale_ref, o_ref, seq_block):
    cur = xc_ref[...].astype(F32)
    prev = jnp.where(seq_block == 0, 0.0, xp_ref[...].astype(F32))
    ext = jnp.concatenate([prev, cur], axis=0)
    t = seq_block * MIX_ROWS + lax.broadcasted_iota(jnp.int32, (MIX_ROWS, 1), 0)
    for gi, w in enumerate(POOL_WINDOWS):
        cols = slice(gi * POOL_GROUP, (gi + 1) * POOL_GROUP)
        s = ext[:, cols]
        span = 1
        while span < w:
            s = s + pltpu.roll(s, span, 0)
            span *= 2
        wsum = s[POOL_HALO:, :]
        count = jnp.minimum(t + 1, w).astype(F32)
        pooled = wsum / count - cur[:, cols]
        mixed = jnp.dot(pooled.astype(BF16), pw_ref[gi], preferred_element_type=F32)
        o_ref[:, cols] = (mixed * scale_ref[:, cols]).astype(o_ref.dtype)


def _attn_fold_mask():
    ci = lax.broadcasted_iota(jnp.int32, (ATTN_BLOCK, ATTN_BLOCK), 0)
    qi = lax.broadcasted_iota(jnp.int32, (ATTN_BLOCK, ATTN_BLOCK), 1)
    return ci, qi, ci > qi


def _attn_init_bias(bias_ref):
    ci, qi, from_prev = _attn_fold_mask()
    dist = jnp.where(from_prev, qi + ATTN_BLOCK - ci, qi - ci).astype(F32)
    for h in range(ATTN_HEADS):
        alibi = -(2.0 ** (-8.0 * (h + 1) / ATTN_HEADS)) * dist
        bias_ref[0, h] = alibi
        bias_ref[1, h] = jnp.where(from_prev, -jnp.inf, alibi)


def _attn_step(sink_ref, q_ref, kc_ref, kp_ref, vc_ref, vp_ref, o_ref,
               bias_ref, s_ref, e_ref, denom_ref, seq_block):
    _, _, from_prev = _attn_fold_mask()
    seq_start = (seq_block == 0).astype(jnp.int32)
    q = q_ref[...] * (ATTN_HEAD_DIM ** -0.5)
    kk = jnp.concatenate([kp_ref[...], kc_ref[...]], axis=0)
    vv = jnp.concatenate([vp_ref[...], vc_ref[...]], axis=0)
    vv_t = vv.astype(F32).T.astype(BF16)

    for h in range(ATTN_HEADS):
        kv_cols = slice((h // ATTN_GROUP) * ATTN_HEAD_DIM, (h // ATTN_GROUP + 1) * ATTN_HEAD_DIM)
        s2 = lax.dot_general(kk[:, kv_cols], q[:, h * ATTN_HEAD_DIM:(h + 1) * ATTN_HEAD_DIM],
                             (((1,), (1,)), ((), ())), preferred_element_type=F32)
        s_ref[h] = jnp.where(from_prev, s2[:ATTN_BLOCK], s2[ATTN_BLOCK:]) + bias_ref[seq_start, h]

    s = s_ref[...]
    sink = sink_ref[...]
    m = jnp.maximum(jnp.max(s, axis=1, keepdims=True), sink)
    e = jnp.exp(s - m)
    denom_ref[...] = jnp.sum(e, axis=1, keepdims=True) + jnp.exp(sink - m)
    eb = e.astype(BF16)
    zero = jnp.zeros_like(eb)
    e_ref[:, :ATTN_BLOCK, :] = jnp.where(from_prev, eb, zero)
    e_ref[:, ATTN_BLOCK:, :] = jnp.where(from_prev, zero, eb)

    for pair in range(ATTN_HEADS // 2):
        outs = []
        for h in (2 * pair, 2 * pair + 1):
            kv_rows = slice((h // ATTN_GROUP) * ATTN_HEAD_DIM, (h // ATTN_GROUP + 1) * ATTN_HEAD_DIM)
            pv_t = jnp.dot(vv_t[kv_rows], e_ref[h], preferred_element_type=F32)
            outs.append(pv_t / denom_ref[h])
        pair_cols = slice(2 * pair * ATTN_HEAD_DIM, (2 * pair + 2) * ATTN_HEAD_DIM)
        o_ref[:, pair_cols] = jnp.concatenate(outs, axis=0).T.astype(o_ref.dtype)


def _ret_init_tables(logg_ref, intra_ref, qdec_ref, kdec_ref):
    key_scale = RET_KEY_DIM ** -0.5
    pos = lax.broadcasted_iota(jnp.int32, (MIX_ROWS, 1), 0).astype(F32)
    row = lax.broadcasted_iota(jnp.int32, (MIX_ROWS, MIX_ROWS), 0)
    col = lax.broadcasted_iota(jnp.int32, (MIX_ROWS, MIX_ROWS), 1)
    diff = (row - col).astype(F32)
    for h in range(RET_HEADS):
        lg = logg_ref[h]
        intra_ref[h] = jnp.where(diff >= 0, jnp.exp(lg * jnp.maximum(diff, 0.0)), 0.0) * key_scale
        qdec_ref[h] = jnp.exp(lg * (pos + 1.0))
        kdec_ref[h] = jnp.exp(lg * (MIX_ROWS - 1.0 - pos)) * key_scale


def _ret_step(logg_ref, q_ref, k_ref, v_ref, g_ref, norm_ref, o_ref,
              state_ref, intra_ref, qdec_ref, kdec_ref):
    for h in range(RET_HEADS):
        k_cols = slice(h * RET_KEY_DIM, (h + 1) * RET_KEY_DIM)
        v_cols = slice(h * RET_VALUE_DIM, (h + 1) * RET_VALUE_DIM)
        chunk_decay = jnp.exp(jnp.full((1, 1), MIX_ROWS, F32) * logg_ref[h])
        q = q_ref[:, k_cols]
        k = k_ref[:, k_cols]
        v = v_ref[:, v_cols]
        state = state_ref[h]
        inner = lax.dot_general(q, k, (((1,), (1,)), ((), ())), preferred_element_type=F32) * intra_ref[h]
        qd = (q.astype(F32) * qdec_ref[h]).astype(BF16)
        o = (jnp.dot(inner.astype(BF16), v, preferred_element_type=F32)
             + jnp.dot(qd, state.astype(BF16), preferred_element_type=F32))
        kd = (k.astype(F32) * kdec_ref[h]).astype(BF16)
        state_ref[h] = state * chunk_decay + lax.dot_general(
            kd, v, (((0,), (0,)), ((), ())), preferred_element_type=F32)

        o = o * _rms_scale(o) * norm_ref[:, v_cols]
        g = g_ref[:, v_cols].astype(F32)
        o_ref[:, v_cols] = (g * jax.nn.sigmoid(g) * o).astype(o_ref.dtype)


def _mix_kernel(logg_ref, h_ref, pre_ref, wgate_ref,
                rq_ref, rk_ref, rv_ref, rg_ref, norm_ref,
                sink_ref, aq_ref, kc_ref, kp_ref, vc_ref, vp_ref,
                xc_ref, xp_ref, pw_ref, pscale_ref,
                gates_ref, rt_ref, at_ref, pm_ref,
                xn_ref, state_ref, intra_ref, qdec_ref, kdec_ref,
                bias_ref, s_ref, e_ref, denom_ref):
    i = pl.program_id(0)
    j = pl.program_id(1)
    seq_block = (i * MIX_STEPS + j) % (SEQ // MIX_ROWS)

    @pl.when((i == 0) & (j == 0))
    def _():
        _ret_init_tables(logg_ref, intra_ref, qdec_ref, kdec_ref)
        _attn_init_bias(bias_ref)

    @pl.when(seq_block == 0)
    def _():
        state_ref[...] = jnp.zeros_like(state_ref)

    def step():
        gates_ref[...] = jnp.dot(xn_ref[...], wgate_ref[...],
                                 preferred_element_type=F32).astype(gates_ref.dtype)
        _ret_step(logg_ref, rq_ref, rk_ref, rv_ref, rg_ref, norm_ref, rt_ref,
                  state_ref, intra_ref, qdec_ref, kdec_ref)
        _attn_step(sink_ref, aq_ref, kc_ref, kp_ref, vc_ref, vp_ref, at_ref,
                   bias_ref, s_ref, e_ref, denom_ref, seq_block)
        _pool_step(xc_ref, xp_ref, pw_ref, pscale_ref, pm_ref, seq_block)

    @pl.when(j == 0)
    def _():
        x = h_ref[...]
        xn_ref[...] = (x * _rms_scale(x) * pre_ref[...]).astype(BF16)
        step()

    pl.when(j > 0)(step)


def _mix(h, pre, w_gate, p, log_g, ret_norm, sinks, pool_w, pool_scale, layer):
    def row(i, j):
        return i * MIX_STEPS + j

    def prev_row(i, j):
        return jnp.maximum(row(i, j) - 1, 0)

    def slab(width, offset):
        return pl.BlockSpec((MIX_ROWS, width), lambda i, j: (row(i, j), offset // width))

    halo_blocks = MIX_ROWS // POOL_HALO
    kcol = OFF_K // KV_WIDTH
    vcol = OFF_V // KV_WIDTH
    in_specs = [
        pl.BlockSpec(memory_space=pltpu.SMEM),
        pl.BlockSpec((MIX_TM, D_MODEL), lambda i, j: (i, 0)),
        pl.BlockSpec((None, 1, D_MODEL), lambda i, j: (layer, 0, 0)),
        pl.BlockSpec((D_MODEL, GATE_TN), lambda i, j: (0, j)),
        slab(RQK_WIDTH, OFF_RQ), slab(RQK_WIDTH, OFF_RK), slab(RV_WIDTH, OFF_RV), slab(RV_WIDTH, OFF_RG),
        pl.BlockSpec((None, 1, RV_WIDTH), lambda i, j: (layer, 0, 0)),
        pl.BlockSpec((None, ATTN_HEADS, 1, 1), lambda i, j: (layer, 0, 0, 0)),
        slab(Q_WIDTH, OFF_Q),
        pl.BlockSpec((MIX_ROWS, KV_WIDTH), lambda i, j: (row(i, j), kcol)),
        pl.BlockSpec((MIX_ROWS, KV_WIDTH), lambda i, j: (prev_row(i, j), kcol)),
        pl.BlockSpec((MIX_ROWS, KV_WIDTH), lambda i, j: (row(i, j), vcol)),
        pl.BlockSpec((MIX_ROWS, KV_WIDTH), lambda i, j: (prev_row(i, j), vcol)),
        slab(POOL_WIDTH, OFF_XP),
        pl.BlockSpec((POOL_HALO, POOL_WIDTH),
                     lambda i, j: (jnp.maximum(row(i, j) * halo_blocks - 1, 0), 0)),
        pl.BlockSpec((None, len(POOL_WINDOWS), POOL_GROUP, POOL_GROUP), lambda i, j: (layer, 0, 0, 0)),
        pl.BlockSpec((None, 1, POOL_WIDTH), lambda i, j: (layer, 0, 0)),
    ]
    return pl.pallas_call(
        _mix_kernel,
        grid=(TOKENS // MIX_TM, MIX_STEPS),
        in_specs=in_specs,
        out_specs=[
            pl.BlockSpec((MIX_TM, GATE_TN), lambda i, j: (i, j)),
            pl.BlockSpec((MIX_ROWS, RV_WIDTH), lambda i, j: (row(i, j), 0)),
            pl.BlockSpec((MIX_ROWS, Q_WIDTH), lambda i, j: (row(i, j), 0)),
            pl.BlockSpec((MIX_ROWS, POOL_WIDTH), lambda i, j: (row(i, j), 0)),
        ],
        out_shape=[
            jax.ShapeDtypeStruct((TOKENS, GATE_WIDTH), BF16),
            jax.ShapeDtypeStruct((TOKENS, RV_WIDTH), BF16),
            jax.ShapeDtypeStruct((TOKENS, Q_WIDTH), BF16),
            jax.ShapeDtypeStruct((TOKENS, POOL_WIDTH), BF16),
        ],
        scratch_shapes=[
            pltpu.VMEM((MIX_TM, D_MODEL), BF16),
            pltpu.VMEM((RET_HEADS, RET_KEY_DIM, RET_VALUE_DIM), F32),
            pltpu.VMEM((RET_HEADS, MIX_ROWS, MIX_ROWS), F32),
            pltpu.VMEM((RET_HEADS, MIX_ROWS, 1), F32),
            pltpu.VMEM((RET_HEADS, MIX_ROWS, 1), F32),
            pltpu.VMEM((2, ATTN_HEADS, ATTN_BLOCK, ATTN_BLOCK), F32),
            pltpu.VMEM((ATTN_HEADS, ATTN_BLOCK, ATTN_BLOCK), F32),
            pltpu.VMEM((ATTN_HEADS, 2 * ATTN_BLOCK, ATTN_BLOCK), BF16),
            pltpu.VMEM((ATTN_HEADS, 1, ATTN_BLOCK), F32),
        ],
        compiler_params=_params("arbitrary", "arbitrary"),
        name="mix",
    )(log_g, h, pre, w_gate, p, p, p, p, ret_norm, sinks,
      p, p, p, p, p, p, p, pool_w, pool_scale)


def _merge_kernel(h_ref, pm_ref, at_ref, rt_ref, gates_ref,
                  wp_ref, wa_ref, wr_ref, wo_ref, post_ref, o_ref, acc_ref, scale_ref):
    def gate(branch, cols):
        return jax.nn.sigmoid(gates_ref[:, branch * D_MODEL + cols.start:branch * D_MODEL + cols.stop]
                              .astype(F32))

    for jt in range(D_MODEL // MERGE_TJ):
        cols = slice(jt * MERGE_TJ, (jt + 1) * MERGE_TJ)
        merged = (gate(0, cols) * jnp.dot(pm_ref[...], wp_ref[:, cols], preferred_element_type=F32)
                  + gate(1, cols) * jnp.dot(at_ref[...], wa_ref[:, cols], preferred_element_type=F32)
                  + gate(2, cols) * jnp.dot(rt_ref[...], wr_ref[:, cols], preferred_element_type=F32))
        part = jnp.dot(merged.astype(BF16), wo_ref[cols, :], preferred_element_type=F32)
        if jt == 0:
            acc_ref[...] = part
        else:
            acc_ref[...] += part

    scale_ref[...] = _rms_scale(acc_ref[...])
    o_ref[...] = h_ref[...] + acc_ref[...] * scale_ref[...] * post_ref[...]


def _merge(h, pm, at, rt, gates, w_pool_out, w_attn_out, w_ret_out, w_out, post, layer):
    def resident(rows, cols):
        return pl.BlockSpec((rows, cols), lambda i: (0, 0), pipeline_mode=pl.Buffered(1))

    return pl.pallas_call(
        _merge_kernel,
        grid=(TOKENS // MERGE_TM,),
        in_specs=[
            pl.BlockSpec((MERGE_TM, D_MODEL), lambda i: (i, 0)),
            pl.BlockSpec((MERGE_TM, POOL_WIDTH), lambda i: (i, 0)),
            pl.BlockSpec((MERGE_TM, Q_WIDTH), lambda i: (i, 0)),
            pl.BlockSpec((MERGE_TM, RV_WIDTH), lambda i: (i, 0)),
            pl.BlockSpec((MERGE_TM, GATE_WIDTH), lambda i: (i, 0)),
            resident(POOL_WIDTH, D_MODEL),
            resident(Q_WIDTH, D_MODEL),
            resident(RV_WIDTH, D_MODEL),
            resident(D_MODEL, D_MODEL),
            pl.BlockSpec((None, 1, D_MODEL), lambda i: (layer, 0, 0)),
        ],
        out_specs=pl.BlockSpec((MERGE_TM, D_MODEL), lambda i: (i, 0)),
        out_shape=jax.ShapeDtypeStruct((TOKENS, D_MODEL), F32),
        scratch_shapes=[pltpu.VMEM((MERGE_TM, D_MODEL), F32), pltpu.VMEM((MERGE_TM, 1), F32)],
        compiler_params=_params("parallel"),
        name="merge",
    )(h, pm, at, rt, gates, w_pool_out, w_attn_out, w_ret_out, w_out, post)


def kernel(x, ffn1_pre, ffn1_up, ffn1_down, ffn1_post, mix_pre, w_in, pool_w, pool_scale, attn_sinks, ret_norm, w_pool_out, w_attn_out, w_ret_out, w_gate, w_out, mix_post, ffn2_pre, ffn2_up, ffn2_down, ffn2_post):
    def gain(a):
        return a.reshape(DEPTH, 1, a.shape[-1])

    pool_w = pool_w.astype(BF16)
    log_g = jnp.log(1.0 - jnp.exp2(-5.0 - jnp.arange(RET_HEADS, dtype=F32)))
    sinks = attn_sinks.reshape(DEPTH, ATTN_HEADS, 1, 1)

    h = x.reshape(TOKENS, D_MODEL)
    for l in range(DEPTH):
        h, (w_gate_bf, w_pool_bf, w_attn_bf, w_ret_bf, w_out_bf) = _ffn(
            h, gain(ffn1_pre), ffn1_up, ffn1_down, gain(ffn1_post), l,
            round_also=(w_gate, w_pool_out, w_attn_out, w_ret_out, w_out))
        p = _proj(h, gain(mix_pre), w_in, l)
        gates, rt, at, pm = _mix(h, gain(mix_pre), w_gate_bf, p, log_g, gain(ret_norm), sinks,
                                 pool_w, gain(pool_scale), l)
        h = _merge(h, pm, at, rt, gates, w_pool_bf, w_attn_bf, w_ret_bf, w_out_bf, gain(mix_post), l)
        h, _ = _ffn(h, gain(ffn2_pre), ffn2_up, ffn2_down, gain(ffn2_post), l)
    return h.reshape(BATCH, SEQ, D_MODEL)
```
